```python
import jax
import jax.numpy as jnp
from jax import lax
import numpy as np

D_MODEL = 2048
BATCH = 2
SEQ = 4096
DEPTH = 1
DEC_BATCH = 16
DEC_SEQ = 16
PAST_LEN = 1024

CHUNK = 64
Q_BLOCK = 128
SB_HEADS = 8
SB_HEAD_DIM = 128
SB_WIDTH = SB_HEADS * SB_HEAD_DIM
MLA_HEADS = 8
Q_LORA = 512
KV_LORA = 512
NOPE_DIM = 128
ROPE_DIM = 64
V_DIM = 128
MLA_WIDTH = MLA_HEADS * V_DIM
ROPE_THETA = 10000.0
D_FF = 5504
LN_EPS = 1e-5
RMS_EPS = 1e-6
ALPHA = (2 * DEPTH) ** 0.25
DN_BETA = (8 * DEPTH) ** -0.25
SB_SCALE = SB_HEAD_DIM ** -0.5
MLA_SCALE = (NOPE_DIM + ROPE_DIM) ** -0.5
IN_SPLITS = (SB_WIDTH, 2 * SB_WIDTH, 3 * SB_WIDTH, 3 * SB_WIDTH + Q_LORA,
             3 * SB_WIDTH + Q_LORA + KV_LORA, 3 * SB_WIDTH + Q_LORA + KV_LORA + ROPE_DIM)
IN_COLS = IN_SPLITS[-1] + 2 * D_MODEL

kernel_name = 'stickbreak_mla_macaron_deepnorm_stream_step'


def layer_norm(x, g, b):
    xf = x.astype(jnp.float32)
    mu = jnp.mean(xf, -1, keepdims=True)
    var = jnp.mean(jnp.square(xf - mu), -1, keepdims=True)
    y = (xf - mu) * lax.rsqrt(var + LN_EPS) * g.astype(jnp.float32) + b.astype(jnp.float32)
    return y.astype(x.dtype)


def rms_norm(x, g):
    xf = x.astype(jnp.float32)
    y = xf * lax.rsqrt(jnp.mean(jnp.square(xf), -1, keepdims=True) + RMS_EPS) * g.astype(jnp.float32)
    return y.astype(x.dtype)


def rope(x, pos):
    half = ROPE_DIM // 2
    inv_freq = ROPE_THETA ** (-jnp.arange(0, ROPE_DIM, 2, dtype=jnp.float32) / ROPE_DIM)
    ang = pos.astype(jnp.float32)[:, None] * inv_freq[None, :]
    ang = ang.reshape(ang.shape[0], *([1] * (x.ndim - 3)), half)
    cos, sin = jnp.cos(ang), jnp.sin(ang)
    xf = x.astype(jnp.float32)
    x1, x2 = xf[..., :half], xf[..., half:]
    return jnp.concatenate([x1 * cos - x2 * sin, x2 * cos + x1 * sin], -1).astype(x.dtype)


def swiglu(x, w_in, w_out):
    gate, up = jnp.split(x @ w_in, 2, axis=-1)
    return (jax.nn.silu(gate) * up) @ w_out


def stick_breaking(q, k, v, q_pos, k_pos):
    z = jnp.einsum('bqhd,bkhd->bhqk', q.astype(jnp.float32), k.astype(jnp.float32)) * SB_SCALE
    visible = k_pos[None, :] < q_pos[:, None]
    log_1m = jnp.where(visible, jax.nn.log_sigmoid(-z), 0.0)
    tail = lax.cumsum(log_1m, axis=3, reverse=True) - log_1m
    a = jnp.where(visible, jnp.exp(jax.nn.log_sigmoid(z) + tail), 0.0)
    return jnp.einsum('bhqk,bkhd->bqhd', a, v.astype(jnp.float32))


def mla_attend(q_nope, q_rope, k_nope, k_rope, v, q_pos, k_pos):
    s = (jnp.einsum('bqhd,bkhd->bhqk', q_nope.astype(jnp.float32), k_nope.astype(jnp.float32))
         + jnp.einsum('bqhr,bkr->bhqk', q_rope.astype(jnp.float32), k_rope.astype(jnp.float32))) * MLA_SCALE
    visible = (k_pos[None, :] // CHUNK) <= (q_pos[:, None] // CHUNK)
    s = jnp.where(visible, s, jnp.finfo(jnp.float32).min)
    p = jax.nn.softmax(s, axis=-1)
    return jnp.einsum('bhqk,bkhd->bqhd', p, v.astype(jnp.float32))


def map_query_blocks(attend, q_parts, q_pos):
    n_blocks = q_pos.shape[0] // Q_BLOCK

    def to_blocks(a):
        a = a.reshape(a.shape[0], n_blocks, Q_BLOCK, *a.shape[2:])
        return jnp.moveaxis(a, 1, 0)

    xs = tuple(to_blocks(a) for a in q_parts) + (q_pos.reshape(n_blocks, Q_BLOCK),)
    out = lax.map(lambda blk: attend(*blk), xs)
    out = jnp.moveaxis(out, 0, 1)
    return out.reshape(out.shape[0], n_blocks * Q_BLOCK, *out.shape[3:])


def token_mixing(h, pos, past, past_pos, p, blocked):
    past_k, past_v, past_ckv, past_kr = past
    B, T, _ = h.shape
    u = h @ p['w_in']
    q_sb, k_sb, v_sb, c_q, c_kv, k_r, gates = jnp.split(u, IN_SPLITS, axis=-1)
    q_sb = q_sb.reshape(B, T, SB_HEADS, SB_HEAD_DIM)
    k_sb = k_sb.reshape(B, T, SB_HEADS, SB_HEAD_DIM)
    v_sb = v_sb.reshape(B, T, SB_HEADS, SB_HEAD_DIM)
    q_mla = (rms_norm(c_q, p['g_cq']) @ p['w_uq']).reshape(B, T, MLA_HEADS, NOPE_DIM + ROPE_DIM)
    q_nope, q_rope = q_mla[..., :NOPE_DIM], rope(q_mla[..., NOPE_DIM:], pos)
    c_kv = rms_norm(c_kv, p['g_ckv'])
    k_r = rope(k_r, pos)
    k_pos = jnp.concatenate([past_pos, pos])
    k_all = jnp.concatenate([past_k, k_sb], axis=1)
    v_all = jnp.concatenate([past_v, v_sb], axis=1)
    ckv_all = jnp.concatenate([past_ckv, c_kv], axis=1)
    kr_all = jnp.concatenate([past_kr, k_r], axis=1)
    kv = (ckv_all @ p['w_ukv']).reshape(B, k_pos.shape[0], MLA_HEADS, NOPE_DIM + V_DIM)
    k_nope, v_mla = kv[..., :NOPE_DIM], kv[..., NOPE_DIM:]

    sb_fn = lambda q, qp: stick_breaking(q, k_all, v_all, qp, k_pos)
    mla_fn = lambda qn, qr, qp: mla_attend(qn, qr, k_nope, kr_all, v_mla, qp, k_pos)
    if blocked:
        o_sb = map_query_blocks(sb_fn, (q_sb,), pos)
        o_mla = map_query_blocks(mla_fn, (q_nope, q_rope), pos)
    else:
        o_sb = sb_fn(q_sb, pos)
        o_mla = mla_fn(q_nope, q_rope, pos)
    o_sb = o_sb.reshape(B, T, SB_WIDTH).astype(h.dtype)
    o_mla = o_mla.reshape(B, T, MLA_WIDTH).astype(h.dtype)
    g = jax.nn.sigmoid((gates + p['b_gate']).astype(jnp.float32))
    g_sb, g_mla = g[..., :D_MODEL], g[..., D_MODEL:]
    merged = g_sb * (o_sb @ p['w_br_sb']).astype(jnp.float32) + g_mla * (o_mla @ p['w_br_mla']).astype(jnp.float32)
    out = merged.astype(h.dtype) @ p['w_o']
    return out, (k_sb, v_sb, c_kv, k_r)


def encoder_layer(x, pos, past, past_pos, p, blocked):
    h = layer_norm(ALPHA * x + 0.5 * swiglu(x, p['ffn1_w_in'], p['ffn1_w_out']), p['ln1_g'], p['ln1_b'])
    mix, new_rows = token_mixing(h, pos, past, past_pos, p, blocked)
    h = layer_norm(ALPHA * h + mix, p['ln2_g'], p['ln2_b'])
    h = layer_norm(ALPHA * h + 0.5 * swiglu(h, p['ffn2_w_in'], p['ffn2_w_out']), p['ln3_g'], p['ln3_b'])
    return h, new_rows


def setup_inputs(seed: int = 0) -> dict:
    key = jax.random.key(seed)
    ks = jax.random.split(key, 32)
    f32 = jnp.float32
    L = DEPTH

    def nrm(k, shape, scale=1.0):
        return jax.random.normal(k, shape, f32) * scale

    return {
        'x_prompt': nrm(ks[0], (BATCH, SEQ, D_MODEL)),
        'x_sample': nrm(ks[1], (DEC_BATCH, DEC_SEQ, D_MODEL)),
        'cache_sb_k': nrm(ks[2], (L, DEC_BATCH, PAST_LEN, SB_HEADS, SB_HEAD_DIM)),
        'cache_sb_v': nrm(ks[3], (L, DEC_BATCH, PAST_LEN, SB_HEADS, SB_HEAD_DIM)),
        'cache_mla_ckv': nrm(ks[4], (L, DEC_BATCH, PAST_LEN, KV_LORA)),
        'cache_mla_krope': nrm(ks[5], (L, DEC_BATCH, PAST_LEN, ROPE_DIM)),
        'ffn1_w_in': nrm(ks[6], (L, D_MODEL, 2 * D_FF), D_MODEL ** -0.5),
        'ffn1_w_out': nrm(ks[7], (L, D_FF, D_MODEL), DN_BETA * D_FF ** -0.5),
        'ln1_g': 1.0 + nrm(ks[8], (L, D_MODEL), 0.02),
        'ln1_b': nrm(ks[9], (L, D_MODEL), 0.02),
        'w_in': nrm(ks[10], (L, D_MODEL, IN_COLS), D_MODEL ** -0.5),
        'b_gate': nrm(ks[11], (L, 2 * D_MODEL), 0.02),
        'g_cq': 1.0 + nrm(ks[12], (L, Q_LORA), 0.02),
        'w_uq': nrm(ks[13], (L, Q_LORA, MLA_HEADS * (NOPE_DIM + ROPE_DIM)), Q_LORA ** -0.5),
        'g_ckv': 1.0 + nrm(ks[14], (L, KV_LORA), 0.02),
        'w_ukv': nrm(ks[15], (L, KV_LORA, MLA_HEADS * (NOPE_DIM + V_DIM)), KV_LORA ** -0.5),
        'w_br_sb': nrm(ks[16], (L, SB_WIDTH, D_MODEL), DN_BETA * SB_WIDTH ** -0.5),
        'w_br_mla': nrm(ks[17], (L, MLA_WIDTH, D_MODEL), DN_BETA * MLA_WIDTH ** -0.5),
        'w_o': nrm(ks[18], (L, D_MODEL, D_MODEL), DN_BETA * D_MODEL ** -0.5),
        'ln2_g': 1.0 + nrm(ks[19], (L, D_MODEL), 0.02),
        'ln2_b': nrm(ks[20], (L, D_MODEL), 0.02),
        'ffn2_w_in': nrm(ks[21], (L, D_MODEL, 2 * D_FF), D_MODEL ** -0.5),
        'ffn2_w_out': nrm(ks[22], (L, D_FF, D_MODEL), DN_BETA * D_FF ** -0.5),
        'ln3_g': 1.0 + nrm(ks[23], (L, D_MODEL), 0.02),
        'ln3_b': nrm(ks[24], (L, D_MODEL), 0.02),
    }


def reference(x_prompt, x_sample, cache_sb_k, cache_sb_v, cache_mla_ckv, cache_mla_krope,
              ffn1_w_in, ffn1_w_out, ln1_g, ln1_b, w_in, b_gate, g_cq, w_uq, g_ckv, w_ukv,
              w_br_sb, w_br_mla, w_o, ln2_g, ln2_b, ffn2_w_in, ffn2_w_out, ln3_g, ln3_b):
    params = dict(ffn1_w_in=ffn1_w_in, ffn1_w_out=ffn1_w_out, ln1_g=ln1_g, ln1_b=ln1_b,
                  w_in=w_in, b_gate=b_gate, g_cq=g_cq, w_uq=w_uq, g_ckv=g_ckv, w_ukv=w_ukv,
                  w_br_sb=w_br_sb, w_br_mla=w_br_mla, w_o=w_o, ln2_g=ln2_g, ln2_b=ln2_b,
                  ffn2_w_in=ffn2_w_in, ffn2_w_out=ffn2_w_out, ln3_g=ln3_g, ln3_b=ln3_b)
    b_p, t_p = x_prompt.shape[0], x_prompt.shape[1]
    t_s = x_sample.shape[1]
    past_len = cache_sb_k.shape[2]
    pos_p = jnp.arange(t_p, dtype=jnp.int32)
    pos_s = past_len + jnp.arange(t_s, dtype=jnp.int32)
    past_pos_p = jnp.arange(0, dtype=jnp.int32)
    past_pos_s = jnp.arange(past_len, dtype=jnp.int32)
    dt = x_prompt.dtype
    no_past = (jnp.zeros((b_p, 0, SB_HEADS, SB_HEAD_DIM), dt), jnp.zeros((b_p, 0, SB_HEADS, SB_HEAD_DIM), dt),
               jnp.zeros((b_p, 0, KV_LORA), dt), jnp.zeros((b_p, 0, ROPE_DIM), dt))

    h_p, h_s = x_prompt, x_sample
    rows_p = ([], [], [], [])
    rows_s = ([], [], [], [])
    for layer in range(DEPTH):
        p = {name: w[layer] for name, w in params.items()}
        h_p, new_p = encoder_layer(h_p, pos_p, no_past, past_pos_p, p, True)
        past_s = (cache_sb_k[layer], cache_sb_v[layer], cache_mla_ckv[layer], cache_mla_krope[layer])
        h_s, new_s = encoder_layer(h_s, pos_s, past_s, past_pos_s, p, False)
        for acc, r in zip(rows_p, new_p):
            acc.append(r)
        for acc, r in zip(rows_s, new_s):
            acc.append(r)

    return (h_p, h_s,
            jnp.stack(rows_p[0]), jnp.stack(rows_p[1]), jnp.stack(rows_p[2]), jnp.stack(rows_p[3]),
            jnp.stack(rows_s[0]), jnp.stack(rows_s[1]), jnp.stack(rows_s[2]), jnp.stack(rows_s[3]))
```

```python
import functools

import jax
import jax.numpy as jnp
from jax import lax
from jax.experimental import pallas as pl
from jax.experimental.pallas import tpu as pltpu

D_MODEL = 2048
DEPTH = 1
CHUNK = 64
SB_HEADS = 8
SB_HEAD_DIM = 128
SB_WIDTH = SB_HEADS * SB_HEAD_DIM
MLA_HEADS = 8
Q_LORA = 512
KV_LORA = 512
NOPE_DIM = 128
ROPE_DIM = 64
V_DIM = 128
MLA_WIDTH = MLA_HEADS * V_DIM
ROPE_THETA = 10000.0
D_FF = 5504
LN_EPS = 1e-5
RMS_EPS = 1e-6
ALPHA = (2 * DEPTH) ** 0.25
SB_SCALE = SB_HEAD_DIM ** -0.5
MLA_SCALE = (NOPE_DIM + ROPE_DIM) ** -0.5

LANE = 128
FF_CHUNK = 512
D_FF_PAD = 5632
MERGE_CHUNK = 512
Q_TILE = 256
ROW_TILE = 512
ROW_TILE_SMALL = 256
VMEM_LIMIT = 56 * 1024 * 1024

BF16 = jnp.bfloat16
F32 = jnp.float32


def _params(sem):
    return pltpu.CompilerParams(dimension_semantics=sem, vmem_limit_bytes=VMEM_LIMIT)


def _dot(a, b):
    return jnp.dot(a, b, preferred_element_type=F32)


def _dot_nt(a, b):
    return lax.dot_general(a, b, (((1,), (1,)), ((), ())), preferred_element_type=F32)


def _layer_norm(r, g, b):
    mu = jnp.mean(r, axis=-1, keepdims=True)
    d = r - mu
    var = jnp.mean(d * d, axis=-1, keepdims=True)
    return d * lax.rsqrt(var + LN_EPS) * g + b


def _rms_norm(x, g):
    return x * lax.rsqrt(jnp.mean(x * x, axis=-1, keepdims=True) + RMS_EPS) * g


def _ffn_ln_kernel(x_ref, wg_ref, wu_ref, wo_ref, g_ref, b_ref, o_ref, xb_ref, acc_ref):
    j = pl.program_id(1)

    @pl.when(j == 0)
    def _():
        xb_ref[...] = x_ref[...].astype(BF16)
        acc_ref[...] = jnp.zeros_like(acc_ref)

    xb = xb_ref[...]
    gate = _dot(xb, wg_ref[...])
    up = _dot(xb, wu_ref[...])
    act = (gate * jax.nn.sigmoid(gate) * up).astype(BF16)
    acc_ref[...] += _dot(act, wo_ref[...])

    @pl.when(j == pl.num_programs(1) - 1)
    def _():
        r = ALPHA * x_ref[...] + 0.5 * acc_ref[...]
        o_ref[...] = _layer_norm(r, g_ref[...], b_ref[...])


def _ffn_ln(x, wg, wu, wo, g, b, bm):
    m = x.shape[0]
    nj = D_FF_PAD // FF_CHUNK
    return pl.pallas_call(
        _ffn_ln_kernel,
        grid=(m // bm, nj),
        in_specs=[
            pl.BlockSpec((bm, D_MODEL), lambda i, j: (i, 0)),
            pl.BlockSpec((D_MODEL, FF_CHUNK), lambda i, j: (0, j)),
            pl.BlockSpec((D_MODEL, FF_CHUNK), lambda i, j: (0, j)),
            pl.BlockSpec((FF_CHUNK, D_MODEL), lambda i, j: (j, 0)),
            pl.BlockSpec((1, D_MODEL), lambda i, j: (0, 0)),
            pl.BlockSpec((1, D_MODEL), lambda i, j: (0, 0)),
        ],
        out_specs=pl.BlockSpec((bm, D_MODEL), lambda i, j: (i, 0)),
        out_shape=jax.ShapeDtypeStruct((m, D_MODEL), F32),
        scratch_shapes=[pltpu.VMEM((bm, D_MODEL), BF16), pltpu.VMEM((bm, D_MODEL), F32)],
        compiler_params=_params(("parallel", "arbitrary")),
        name="ffn_ln",
    )(x, wg, wu, wo, g, b)


def _qkv_kernel(h_ref, w_ref, q_ref, k_ref, v_ref, kb_ref, vb_ref):
    hb = h_ref[...].astype(BF16)
    q_ref[...] = _dot(hb, w_ref[:, 0:SB_WIDTH]).astype(BF16)
    k = _dot(hb, w_ref[:, SB_WIDTH:2 * SB_WIDTH])
    k_ref[...] = k
    kb_ref[...] = k.astype(BF16)
    v = _dot(hb, w_ref[:, 2 * SB_WIDTH:3 * SB_WIDTH])
    v_ref[...] = v
    vb_ref[...] = v.astype(BF16)


def _qkv_proj(h, w_qkv, bm):
    m = h.shape[0]
    row = lambda i: (i, 0)
    const = lambda i: (0, 0)
    return pl.pallas_call(
        _qkv_kernel,
        grid=(m // bm,),
        in_specs=[pl.BlockSpec((bm, D_MODEL), row), pl.BlockSpec((D_MODEL, 3 * SB_WIDTH), const)],
        out_specs=[pl.BlockSpec((bm, SB_WIDTH), row)] * 5,
        out_shape=[
            jax.ShapeDtypeStruct((m, SB_WIDTH), BF16),
            jax.ShapeDtypeStruct((m, SB_WIDTH), F32),
            jax.ShapeDtypeStruct((m, SB_WIDTH), F32),
            jax.ShapeDtypeStruct((m, SB_WIDTH), BF16),
            jax.ShapeDtypeStruct((m, SB_WIDTH), BF16),
        ],
        compiler_params=_params(("parallel",)),
        name="qkv_proj",
    )(h, w_qkv)


def _rope_lanes(x, cos_t, nsin_lo, sin_hi):
    return x * cos_t + pltpu.roll(x, LANE - ROPE_DIM // 2, 1) * nsin_lo + pltpu.roll(x, ROPE_DIM // 2, 1) * sin_hi


def _lat_kernel(h_ref, wl_ref, gq_ref, gkv_ref, wuq_ref, wuk_ref, wuv_ref, freq_ref,
                qc_ref, ckv_ref, kr_ref, krb_ref, kn_ref, vm_ref, *, bm, period, offset):
    hb = h_ref[...].astype(BF16)
    row = pl.program_id(0) * bm + lax.broadcasted_iota(jnp.int32, (bm, LANE), 0)
    pos = (jnp.bitwise_and(row, period - 1) + offset).astype(F32)
    ang = pos * freq_ref[...]
    lane = lax.broadcasted_iota(jnp.int32, (bm, LANE), 1)
    half = ROPE_DIM // 2
    sin_a = jnp.sin(ang)
    cos_t = jnp.where(lane < ROPE_DIM, jnp.cos(ang), 0.0)
    nsin_lo = jnp.where(lane < half, -sin_a, 0.0)
    sin_hi = jnp.where((lane >= half) & (lane < ROPE_DIM), sin_a, 0.0)

    c_q = _rms_norm(_dot(hb, wl_ref[:, 0:Q_LORA]), gq_ref[...]).astype(BF16)
    q_all = _dot(c_q, wuq_ref[...])
    for hd in range(MLA_HEADS):
        base = 2 * LANE * hd
        qc_ref[:, base:base + LANE] = q_all[:, base:base + LANE].astype(BF16)
        qr = _rope_lanes(q_all[:, base + LANE:base + 2 * LANE], cos_t, nsin_lo, sin_hi)
        qc_ref[:, base + LANE:base + 2 * LANE] = qr.astype(BF16)

    c_kv = _rms_norm(_dot(hb, wl_ref[:, Q_LORA:Q_LORA + KV_LORA]), gkv_ref[...])
    ckv_ref[...] = c_kv
    c_kvb = c_kv.astype(BF16)
    kn_ref[...] = _dot(c_kvb, wuk_ref[...]).astype(BF16)
    vm_ref[...] = _dot(c_kvb, wuv_ref[...]).astype(BF16)

    k_r = _rope_lanes(_dot(hb, wl_ref[:, Q_LORA + KV_LORA:Q_LORA + KV_LORA + LANE]), cos_t, nsin_lo, sin_hi)
    kr_ref[...] = k_r[:, 0:ROPE_DIM]
    krb_ref[...] = k_r.astype(BF16)


def _lat_proj(h, w_lat, g_cq, g_ckv, w_uq, w_uk, w_uv, freq, bm, period, offset):
    m = h.shape[0]
    assert period & (period - 1) == 0
    row = lambda i: (i, 0)
    const = lambda i: (0, 0)
    lat_cols = Q_LORA + KV_LORA + LANE
    return pl.pallas_call(
        functools.partial(_lat_kernel, bm=bm, period=period, offset=offset),
        grid=(m // bm,),
        in_specs=[
            pl.BlockSpec((bm, D_MODEL), row),
            pl.BlockSpec((D_MODEL, lat_cols), const),
            pl.BlockSpec((1, Q_LORA), const),
            pl.BlockSpec((1, KV_LORA), const),
            pl.BlockSpec((Q_LORA, 2 * LANE * MLA_HEADS), const),
            pl.BlockSpec((KV_LORA, MLA_HEADS * NOPE_DIM), const),
            pl.BlockSpec((KV_LORA, MLA_WIDTH), const),
            pl.BlockSpec((1, LANE), const),
        ],
        out_specs=[
            pl.BlockSpec((bm, 2 * LANE * MLA_HEADS), row),
            pl.BlockSpec((bm, KV_LORA), row),
            pl.BlockSpec((bm, ROPE_DIM), row),
            pl.BlockSpec((bm, LANE), row),
            pl.BlockSpec((bm, MLA_HEADS * NOPE_DIM), row),
            pl.BlockSpec((bm, MLA_WIDTH), row),
        ],
        out_shape=[
            jax.ShapeDtypeStruct((m, 2 * LANE * MLA_HEADS), BF16),
            jax.ShapeDtypeStruct((m, KV_LORA), F32),
            jax.ShapeDtypeStruct((m, ROPE_DIM), F32),
            jax.ShapeDtypeStruct((m, LANE), BF16),
            jax.ShapeDtypeStruct((m, MLA_HEADS * NOPE_DIM), BF16),
            jax.ShapeDtypeStruct((m, MLA_WIDTH), BF16),
        ],
        compiler_params=_params(("parallel",)),
        name="lat_proj",
    )(h, w_lat, g_cq, g_ckv, w_uq, w_uk, w_uv, freq)


def _strict_upper(n):
    j = lax.broadcasted_iota(jnp.int32, (n, n), 0)
    s = lax.broadcasted_iota(jnp.int32, (n, n), 1)
    return jnp.where(j > s, 1.0, 0.0).astype(BF16)


def _sb_step(q, k, v, u_tri, run, acc, vis):
    z = _dot_nt(q, k) * SB_SCALE
    l1m = -(jnp.maximum(z, 0.0) + jnp.log1p(jnp.exp(-jnp.abs(z))))
    if vis is not None:
        l1m = jnp.where(vis, l1m, 0.0)
    hi = l1m.astype(BF16)
    lo = (l1m - hi.astype(F32)).astype(BF16)
    tail = _dot(hi, u_tri) + _dot(lo, u_tri)
    a = jnp.exp(z + l1m + tail + run)
    if vis is not None:
        a = jnp.where(vis, a, 0.0)
    acc = acc + _dot(a.astype(BF16), v)
    run = run + jnp.sum(l1m, axis=-1, keepdims=True)
    return run, acc


def _sb_prompt_kernel(q_ref, k_ref, v_ref, o_ref, *, tile):
    i = pl.program_id(2)
    q = q_ref[...]
    u_tri = _strict_upper(tile)
    r = lax.broadcasted_iota(jnp.int32, (tile, tile), 0)
    c = lax.broadcasted_iota(jnp.int32, (tile, tile), 1)
    start = pl.multiple_of(i * tile, tile)
    run = jnp.zeros((tile, 1), F32)
    acc = jnp.zeros((tile, SB_HEAD_DIM), F32)
    run, acc = _sb_step(q, k_ref[pl.ds(start, tile), :], v_ref[pl.ds(start, tile), :], u_tri, run, acc, c < r)

    def body(t, carry):
        s0 = pl.multiple_of((i - 1 - t) * tile, tile)
        return _sb_step(q, k_ref[pl.ds(s0, tile), :], v_ref[pl.ds(s0, tile), :], u_tri, carry[0], carry[1], None)

    run, acc = lax.fori_loop(0, i, body, (run, acc))
    o_ref[...] = acc.astype(BF16)


def _sb_prompt(q, k, v, batch, seq):
    nq = seq // Q_TILE
    qmap = lambda b, h, i: (b * nq + i, h)
    kvmap = lambda b, h, i: (b, h)
    return pl.pallas_call(
        functools.partial(_sb_prompt_kernel, tile=Q_TILE),
        grid=(batch, SB_HEADS, nq),
        in_specs=[
            pl.BlockSpec((Q_TILE, SB_HEAD_DIM), qmap),
            pl.BlockSpec((seq, SB_HEAD_DIM), kvmap),
            pl.BlockSpec((seq, SB_HEAD_DIM), kvmap),
        ],
        out_specs=pl.BlockSpec((Q_TILE, SB_HEAD_DIM), qmap),
        out_shape=jax.ShapeDtypeStruct((batch * seq, SB_WIDTH), BF16),
        compiler_params=_params(("parallel", "parallel", "arbitrary")),
        name="sb_prompt",
    )(q, k, v)


def _sb_sample_kernel(q_ref, kn_ref, vn_ref, kc_ref, vc_ref, o_ref, *, t_new, past, tile):
    u_tri = _strict_upper(tile)
    r = lax.broadcasted_iota(jnp.int32, (t_new, tile), 0)
    c = lax.broadcasted_iota(jnp.int32, (t_new, tile), 1)
    zpad = jnp.zeros((tile - t_new, SB_HEAD_DIM), BF16)
    for hd in range(SB_HEADS):
        cols = slice(hd * SB_HEAD_DIM, (hd + 1) * SB_HEAD_DIM)
        q = q_ref[:, cols]
        k_new = jnp.concatenate([kn_ref[:, cols], zpad], axis=0)
        v_new = jnp.concatenate([vn_ref[:, cols], zpad], axis=0)
        run = jnp.zeros((t_new, 1), F32)
        acc = jnp.zeros((t_new, SB_HEAD_DIM), F32)
        run, acc = _sb_step(q, k_new, v_new, u_tri, run, acc, c < r)
        for blk in reversed(range(past // tile)):
            rows = slice(blk * tile, (blk + 1) * tile)
            run, acc = _sb_step(q, kc_ref[0, rows, cols].astype(BF16), vc_ref[0, rows, cols].astype(BF16),
                                u_tri, run, acc, None)
        o_ref[:, cols] = acc.astype(BF16)


def _sb_sample(q, k_new, v_new, cache_k, cache_v, streams, t_new, past):
    row = lambda b: (b, 0)
    cmap = lambda b: (b, 0, 0)
    return pl.pallas_call(
        functools.partial(_sb_sample_kernel, t_new=t_new, past=past, tile=LANE),
        grid=(streams,),
        in_specs=[
            pl.BlockSpec((t_new, SB_WIDTH), row),
            pl.BlockSpec((t_new, SB_WIDTH), row),
            pl.BlockSpec((t_new, SB_WIDTH), row),
            pl.BlockSpec((1, past, SB_WIDTH), cmap),
            pl.BlockSpec((1, past, SB_WIDTH), cmap),
        ],
        out_specs=pl.BlockSpec((t_new, SB_WIDTH), row),
        out_shape=jax.ShapeDtypeStruct((streams * t_new, SB_WIDTH), BF16),
        compiler_params=_params(("parallel",)),
        name="sb_sample",
    )(q, k_new, v_new, cache_k, cache_v)


def _softmax_step(q, kc, v, m, l, acc, vis):
    s = _dot_nt(q, kc) * MLA_SCALE
    if vis is not None:
        s = jnp.where(vis, s, -jnp.inf)
    m_new = jnp.maximum(m, jnp.max(s, axis=-1, keepdims=True))
    corr = jnp.exp(m - m_new)
    p = jnp.exp(s - m_new)
    l = l * corr + jnp.sum(p, axis=-1, keepdims=True)
    acc = acc * corr + _dot(p.astype(BF16), v)
    return m_new, l, acc


def _mla_prompt_kernel(q_ref, kn_ref, kr_ref, v_ref, o_ref, *, tile):
    i = pl.program_id(2)
    q = q_ref[...]

    def keys(s0):
        return jnp.concatenate([kn_ref[pl.ds(s0, tile), :], kr_ref[pl.ds(s0, tile), :]], axis=1)

    def body(t, carry):
        s0 = pl.multiple_of(t * tile, tile)
        return _softmax_step(q, keys(s0), v_ref[pl.ds(s0, tile), :], *carry, None)

    init = (jnp.full((tile, 1), -jnp.inf, F32), jnp.zeros((tile, 1), F32), jnp.zeros((tile, V_DIM), F32))
    carry = lax.fori_loop(0, i, body, init)
    r = lax.broadcasted_iota(jnp.int32, (tile, tile), 0)
    c = lax.broadcasted_iota(jnp.int32, (tile, tile), 1)
    vis = (c // CHUNK) <= (r // CHUNK)
    start = pl.multiple_of(i * tile, tile)
    m, l, acc = _softmax_step(q, keys(start), v_ref[pl.ds(start, tile), :], *carry, vis)
    o_ref[...] = (acc / l).astype(BF16)


def _mla_prompt(q_cat, k_nope, kr_pad, v, batch, seq):
    nq = seq // Q_TILE
    qmap = lambda b, h, i: (b * nq + i, h)
    kvmap = lambda b, h, i: (b, h)
    return pl.pallas_call(
        functools.partial(_mla_prompt_kernel, tile=Q_TILE),
        grid=(batch, MLA_HEADS, nq),
        in_specs=[
            pl.BlockSpec((Q_TILE, 2 * LANE), qmap),
            pl.BlockSpec((seq, NOPE_DIM), kvmap),
            pl.BlockSpec((seq, LANE), lambda b, h, i: (b, 0)),
            pl.BlockSpec((seq, V_DIM), kvmap),
        ],
        out_specs=pl.BlockSpec((Q_TILE, V_DIM), qmap),
        out_shape=jax.ShapeDtypeStruct((batch * seq, MLA_WIDTH), BF16),
        compiler_params=_params(("parallel", "parallel", "arbitrary")),
        name="mla_prompt",
    )(q_cat, k_nope, kr_pad, v)


def _mla_sample_kernel(q_ref, knn_ref, krn_ref, vn_ref, ckv_ref, krc_ref, wuk_ref, wuv_ref, o_ref, *, t_new):
    ckv = ckv_ref[0].astype(BF16)
    k_past = _dot(ckv, wuk_ref[...]).astype(BF16)
    v_past = _dot(ckv, wuv_ref[...]).astype(BF16)
    kr_past = krc_ref[0].astype(BF16)
    kr_new = krn_ref[...]
    for hd in range(MLA_HEADS):
        cols = slice(hd * NOPE_DIM, (hd + 1) * NOPE_DIM)
        q = q_ref[:, 2 * LANE * hd:2 * LANE * (hd + 1)]
        q_nope = q[:, 0:NOPE_DIM]
        q_rope = q[:, NOPE_DIM:NOPE_DIM + ROPE_DIM]
        s_p = (_dot_nt(q_nope, k_past[:, cols]) + _dot_nt(q_rope, kr_past)) * MLA_SCALE
        s_n = _dot_nt(q, jnp.concatenate([knn_ref[:, cols], kr_new], axis=1)) * MLA_SCALE
        m = jnp.maximum(jnp.max(s_p, axis=-1, keepdims=True), jnp.max(s_n, axis=-1, keepdims=True))
        p_p = jnp.exp(s_p - m)
        p_n = jnp.exp(s_n - m)
        l = jnp.sum(p_p, axis=-1, keepdims=True) + jnp.sum(p_n, axis=-1, keepdims=True)
        acc = _dot(p_p.astype(BF16), v_past[:, cols]) + _dot(p_n.astype(BF16), vn_ref[:, cols])
        o_ref[:, cols] = (acc / l).astype(BF16)


def _mla_sample(q_cat, kn_new, kr_new, v_new, cache_ckv, cache_kr, w_uk, w_uv, streams, t_new, past):
    row = lambda b: (b, 0)
    cmap = lambda b: (b, 0, 0)
    const = lambda b: (0, 0)
    return pl.pallas_call(
        functools.partial(_mla_sample_kernel, t_new=t_new),
        grid=(streams,),
        in_specs=[
            pl.BlockSpec((t_new, 2 * LANE * MLA_HEADS), row),
            pl.BlockSpec((t_new, MLA_HEADS * NOPE_DIM), row),
            pl.BlockSpec((t_new, LANE), row),
            pl.BlockSpec((t_new, MLA_WIDTH), row),
            pl.BlockSpec((1, past, KV_LORA), cmap),
            pl.BlockSpec((1, past, ROPE_DIM), cmap),
            pl.BlockSpec((KV_LORA, MLA_HEADS * NOPE_DIM), const),
            pl.BlockSpec((KV_LORA, MLA_WIDTH), const),
        ],
        out_specs=pl.BlockSpec((t_new, MLA_WIDTH), row),
        out_shape=jax.ShapeDtypeStruct((streams * t_new, MLA_WIDTH), BF16),
        compiler_params=_params(("parallel",)),
        name="mla_sample",
    )(q_cat, kn_new, kr_new, v_new, cache_ckv, cache_kr, w_uk, w_uv)


def _merge_ln_kernel(h_ref, osb_ref, omla_ref, wgs_ref, wgm_ref, bgs_ref, bgm_ref, wbs_ref, wbm_ref,
                     wo_ref, g_ref, b_ref, o_ref, hb_ref, mrg_ref):
    c = pl.program_id(1)
    nc = mrg_ref.shape[0]

    @pl.when(c == 0)
    def _():
        hb_ref[...] = h_ref[...].astype(BF16)

    hb = hb_ref[...]
    g_sb = jax.nn.sigmoid(_dot(hb, wgs_ref[...]) + bgs_ref[...])
    g_mla = jax.nn.sigmoid(_dot(hb, wgm_ref[...]) + bgm_ref[...])
    merged = g_sb * _dot(osb_ref[...], wbs_ref[...]) + g_mla * _dot(omla_ref[...], wbm_ref[...])
    mrg_ref[c] = merged.astype(BF16)

    @pl.when(c == nc - 1)
    def _():
        mix = _dot(mrg_ref[0], wo_ref[0:MERGE_CHUNK, :])
        for cc in range(1, nc):
            mix = mix + _dot(mrg_ref[cc], wo_ref[cc * MERGE_CHUNK:(cc + 1) * MERGE_CHUNK, :])
        o_ref[...] = _layer_norm(ALPHA * h_ref[...] + mix, g_ref[...], b_ref[...])


def _merge_ln(h, o_sb, o_mla, w_gs, w_gm, b_gs, b_gm, w_bs, w_bm, w_o, g, b, bm):
    m = h.shape[0]
    nc = D_MODEL // MERGE_CHUNK
    row = lambda i, c: (i, 0)
    col = lambda i, c: (0, c)
    const = lambda i, c: (0, 0)
    return pl.pallas_call(
        _merge_ln_kernel,
        grid=(m // bm, nc),
        in_specs=[
            pl.BlockSpec((bm, D_MODEL), row),
            pl.BlockSpec((bm, SB_WIDTH), row),
            pl.BlockSpec((bm, MLA_WIDTH), row),
            pl.BlockSpec((D_MODEL, MERGE_CHUNK), col),
            pl.BlockSpec((D_MODEL, MERGE_CHUNK), col),
            pl.BlockSpec((1, MERGE_CHUNK), col),
            pl.BlockSpec((1, MERGE_CHUNK), col),
            pl.BlockSpec((SB_WIDTH, MERGE_CHUNK), col),
            pl.BlockSpec((MLA_WIDTH, MERGE_CHUNK), col),
            pl.BlockSpec((D_MODEL, D_MODEL), const),
            pl.BlockSpec((1, D_MODEL), const),
            pl.BlockSpec((1, D_MODEL), const),
        ],
        out_specs=pl.BlockSpec((bm, D_MODEL), row),
        out_shape=jax.ShapeDtypeStruct((m, D_MODEL), F32),
        scratch_shapes=[pltpu.VMEM((bm, D_MODEL), BF16), pltpu.VMEM((nc, bm, MERGE_CHUNK), BF16)],
        compiler_params=_params(("parallel", "arbitrary")),
        name="merge_ln",
    )(h, o_sb, o_mla, w_gs, w_gm, b_gs, b_gm, w_bs, w_bm, w_o, g, b)


def _prep_weights(ffn1_w_in, ffn1_w_out, w_in, b_gate, w_uq, w_ukv, w_br_sb, w_br_mla, w_o, ffn2_w_in, ffn2_w_out):
    def ffn(w_i, w_out):
        pad = D_FF_PAD - D_FF
        wg = jnp.pad(w_i[:, :D_FF].astype(BF16), ((0, 0), (0, pad)))
        wu = jnp.pad(w_i[:, D_FF:].astype(BF16), ((0, 0), (0, pad)))
        wo = jnp.pad(w_out.astype(BF16), ((0, pad), (0, 0)))
        return wg, wu, wo

    c0 = 3 * SB_WIDTH
    c1 = c0 + Q_LORA + KV_LORA + ROPE_DIM
    w_qkv = w_in[:, :c0].astype(BF16)
    w_lat = jnp.pad(w_in[:, c0:c1].astype(BF16), ((0, 0), (0, LANE - ROPE_DIM)))
    w_gs = w_in[:, c1:c1 + D_MODEL].astype(BF16)
    w_gm = w_in[:, c1 + D_MODEL:].astype(BF16)
    b_gs = b_gate[:D_MODEL].reshape(1, D_MODEL)
    b_gm = b_gate[D_MODEL:].reshape(1, D_MODEL)
    w_uq3 = w_uq.astype(BF16).reshape(Q_LORA, MLA_HEADS, NOPE_DIM + ROPE_DIM)
    w_uq_p = jnp.pad(w_uq3, ((0, 0), (0, 0), (0, 2 * LANE - NOPE_DIM - ROPE_DIM))).reshape(Q_LORA, 2 * LANE * MLA_HEADS)
    w_ukv3 = w_ukv.astype(BF16).reshape(KV_LORA, MLA_HEADS, NOPE_DIM + V_DIM)
    w_uk = w_ukv3[:, :, :NOPE_DIM].reshape(KV_LORA, MLA_HEADS * NOPE_DIM)
    w_uv = w_ukv3[:, :, NOPE_DIM:].reshape(KV_LORA, MLA_WIDTH)
    return dict(ffn1=ffn(ffn1_w_in, ffn1_w_out), ffn2=ffn(ffn2_w_in, ffn2_w_out),
                w_qkv=w_qkv, w_lat=w_lat, w_gs=w_gs, w_gm=w_gm, b_gs=b_gs, b_gm=b_gm,
                w_uq=w_uq_p, w_uk=w_uk, w_uv=w_uv,
                w_bs=w_br_sb.astype(BF16), w_bm=w_br_mla.astype(BF16), w_o=w_o.astype(BF16))


def _rope_freq():
    inv_freq = ROPE_THETA ** (-jnp.arange(0, ROPE_DIM, 2, dtype=F32) / ROPE_DIM)
    return jnp.concatenate([inv_freq, inv_freq, jnp.zeros((LANE - ROPE_DIM,), F32)]).reshape(1, LANE)


def _row2(v):
    return v.reshape(1, -1)


def kernel(x_prompt, x_sample, cache_sb_k, cache_sb_v, cache_mla_ckv, cache_mla_krope, ffn1_w_in, ffn1_w_out, ln1_g, ln1_b, w_in, b_gate, g_cq, w_uq, g_ckv, w_ukv, w_br_sb, w_br_mla, w_o, ln2_g, ln2_b, ffn2_w_in, ffn2_w_out, ln3_g, ln3_b):
    assert ffn1_w_in.shape[0] == DEPTH == 1
    b_p, t_p, _ = x_prompt.shape
    b_s, t_s, _ = x_sample.shape
    past = cache_sb_k.shape[2]
    w = _prep_weights(ffn1_w_in[0], ffn1_w_out[0], w_in[0], b_gate[0], w_uq[0], w_ukv[0],
                      w_br_sb[0], w_br_mla[0], w_o[0], ffn2_w_in[0], ffn2_w_out[0])
    freq = _rope_freq()
    ln1 = (_row2(ln1_g[0]), _row2(ln1_b[0]))
    ln2 = (_row2(ln2_g[0]), _row2(ln2_b[0]))
    ln3 = (_row2(ln3_g[0]), _row2(ln3_b[0]))
    g_cq2, g_ckv2 = _row2(g_cq[0]), _row2(g_ckv[0])

    def rowwise_front(x2, bm, period, offset):
        h1 = _ffn_ln(x2, *w["ffn1"], *ln1, bm)
        q, k, v, kb, vb = _qkv_proj(h1, w["w_qkv"], min(bm, ROW_TILE_SMALL))
        qc, ckv, kr, krb, kn, vm = _lat_proj(h1, w["w_lat"], g_cq2, g_ckv2, w["w_uq"], w["w_uk"], w["w_uv"],
                                             freq, bm, period, offset)
        return h1, (q, kb, vb), (qc, kn, krb, vm), (k, v, ckv, kr)

    def rowwise_back(h1, o_sb, o_mla, bm):
        h2 = _merge_ln(h1, o_sb, o_mla, w["w_gs"], w["w_gm"], w["b_gs"], w["b_gm"], w["w_bs"], w["w_bm"],
                       w["w_o"], *ln2, min(bm, ROW_TILE_SMALL))
        return _ffn_ln(h2, *w["ffn2"], *ln3, bm)

    m_p = b_p * t_p
    h1p, (q, kb, vb), (qc, kn, krb, vm), rows_p = rowwise_front(x_prompt.reshape(m_p, D_MODEL), ROW_TILE, t_p, 0)
    o_sb = _sb_prompt(q, kb, vb, b_p, t_p)
    o_mla = _mla_prompt(qc, kn, krb, vm, b_p, t_p)
    y_p = rowwise_back(h1p, o_sb, o_mla, ROW_TILE)

    m_s = b_s * t_s
    h1s, (q, kb, vb), (qc, kn, krb, vm), rows_s = rowwise_front(x_sample.reshape(m_s, D_MODEL), m_s, t_s, past)
    o_sb = _sb_sample(q, kb, vb, cache_sb_k[0].reshape(b_s, past, SB_WIDTH),
                      cache_sb_v[0].reshape(b_s, past, SB_WIDTH), b_s, t_s, past)
    o_mla = _mla_sample(qc, kn, krb, vm, cache_mla_ckv[0], cache_mla_krope[0], w["w_uk"], w["w_uv"], b_s, t_s, past)
    y_s = rowwise_back(h1s, o_sb, o_mla, m_s)

    def cache_rows(rows, b, t):
        k, v, ckv, kr = rows
        return (k.reshape(1, b, t, SB_HEADS, SB_HEAD_DIM), v.reshape(1, b, t, SB_HEADS, SB_HEAD_DIM),
                ckv.reshape(1, b, t, KV_LORA), kr.reshape(1, b, t, ROPE_DIM))

    return (y_p.reshape(b_p, t_p, D_MODEL), y_s.reshape(b_s, t_s, D_MODEL),
            *cache_rows(rows_p, b_p, t_p), *cache_rows(rows_s, b_s, t_s))
```

```python
import functools

import jax
import jax.numpy as jnp
from jax import lax
from jax.experimental import pallas as pl
from jax.experimental.pallas import tpu as pltpu

D_MODEL = 2048
DEPTH = 1
CHUNK = 64
SB_HEADS = 8
SB_HEAD_DIM = 128
SB_WIDTH = SB_HEADS * SB_HEAD_DIM
MLA_HEADS = 8
Q_LORA = 512
KV_LORA = 512
NOPE_DIM = 128
ROPE_DIM = 64
V_DIM = 128
MLA_WIDTH = MLA_HEADS * V_DIM
ROPE_THETA = 10000.0
D_FF = 5504
LN_EPS = 1e-5
RMS_EPS = 1e-6
ALPHA = (2 * DEPTH) ** 0.25
SB_SCALE = SB_HEAD_DIM ** -0.5
MLA_SCALE = (NOPE_DIM + ROPE_DIM) ** -0.5

LANE = 128
FF_CHUNK = 512
D_FF_PAD = 5632
MERGE_CHUNK = 512
Q_TILE = 256
MLA_TILE = 512
SB_EXP_UNDERFLOW = 120.0
ROW_TILE = 512
ROW_TILE_SMALL = 256
VMEM_LIMIT = 56 * 1024 * 1024

BF16 = jnp.bfloat16
F32 = jnp.float32


def _params(sem):
    return pltpu.CompilerParams(dimension_semantics=sem, vmem_limit_bytes=VMEM_LIMIT)


def _dot(a, b):
    return jnp.dot(a, b, preferred_element_type=F32)


def _dot_nt(a, b):
    return lax.dot_general(a, b, (((1,), (1,)), ((), ())), preferred_element_type=F32)


def _layer_norm(r, g, b):
    mu = jnp.mean(r, axis=-1, keepdims=True)
    d = r - mu
    var = jnp.mean(d * d, axis=-1, keepdims=True)
    return d * lax.rsqrt(var + LN_EPS) * g + b


def _rms_norm(x, g):
    return x * lax.rsqrt(jnp.mean(x * x, axis=-1, keepdims=True) + RMS_EPS) * g


def _ffn_ln_kernel(x_ref, wg_ref, wu_ref, wo_ref, g_ref, b_ref, o_ref, xb_ref, acc_ref):
    j = pl.program_id(1)

    @pl.when(j == 0)
    def _():
        xb_ref[...] = x_ref[...].astype(BF16)
        acc_ref[...] = jnp.zeros_like(acc_ref)

    xb = xb_ref[...]
    gate = _dot(xb, wg_ref[...])
    up = _dot(xb, wu_ref[...])
    act = (gate * jax.nn.sigmoid(gate) * up).astype(BF16)
    acc_ref[...] += _dot(act, wo_ref[...])

    @pl.when(j == pl.num_programs(1) - 1)
    def _():
        r = ALPHA * x_ref[...] + 0.5 * acc_ref[...]
        o_ref[...] = _layer_norm(r, g_ref[...], b_ref[...])


def _ffn_ln(x, wg, wu, wo, g, b, bm):
    m = x.shape[0]
    nj = D_FF_PAD // FF_CHUNK
    return pl.pallas_call(
        _ffn_ln_kernel,
        grid=(m // bm, nj),
        in_specs=[
            pl.BlockSpec((bm, D_MODEL), lambda i, j: (i, 0)),
            pl.BlockSpec((D_MODEL, FF_CHUNK), lambda i, j: (0, j)),
            pl.BlockSpec((D_MODEL, FF_CHUNK), lambda i, j: (0, j)),
            pl.BlockSpec((FF_CHUNK, D_MODEL), lambda i, j: (j, 0)),
            pl.BlockSpec((1, D_MODEL), lambda i, j: (0, 0)),
            pl.BlockSpec((1, D_MODEL), lambda i, j: (0, 0)),
        ],
        out_specs=pl.BlockSpec((bm, D_MODEL), lambda i, j: (i, 0)),
        out_shape=jax.ShapeDtypeStruct((m, D_MODEL), F32),
        scratch_shapes=[pltpu.VMEM((bm, D_MODEL), BF16), pltpu.VMEM((bm, D_MODEL), F32)],
        compiler_params=_params(("parallel", "arbitrary")),
        name="ffn_ln",
    )(x, wg, wu, wo, g, b)


def _qkv_kernel(h_ref, w_ref, q_ref, k_ref, v_ref, kb_ref, vb_ref):
    hb = h_ref[...].astype(BF16)
    q_ref[...] = _dot(hb, w_ref[:, 0:SB_WIDTH]).astype(BF16)
    k = _dot(hb, w_ref[:, SB_WIDTH:2 * SB_WIDTH])
    k_ref[...] = k
    kb_ref[...] = k.astype(BF16)
    v = _dot(hb, w_ref[:, 2 * SB_WIDTH:3 * SB_WIDTH])
    v_ref[...] = v
    vb_ref[...] = v.astype(BF16)


def _qkv_proj(h, w_qkv, bm):
    m = h.shape[0]
    row = lambda i: (i, 0)
    const = lambda i: (0, 0)
    return pl.pallas_call(
        _qkv_kernel,
        grid=(m // bm,),
        in_specs=[pl.BlockSpec((bm, D_MODEL), row), pl.BlockSpec((D_MODEL, 3 * SB_WIDTH), const)],
        out_specs=[pl.BlockSpec((bm, SB_WIDTH), row)] * 5,
        out_shape=[
            jax.ShapeDtypeStruct((m, SB_WIDTH), BF16),
            jax.ShapeDtypeStruct((m, SB_WIDTH), F32),
            jax.ShapeDtypeStruct((m, SB_WIDTH), F32),
            jax.ShapeDtypeStruct((m, SB_WIDTH), BF16),
            jax.ShapeDtypeStruct((m, SB_WIDTH), BF16),
        ],
        compiler_params=_params(("parallel",)),
        name="qkv_proj",
    )(h, w_qkv)


def _rope_lanes(x, cos_t, nsin_lo, sin_hi):
    return x * cos_t + pltpu.roll(x, LANE - ROPE_DIM // 2, 1) * nsin_lo + pltpu.roll(x, ROPE_DIM // 2, 1) * sin_hi


def _lat_kernel(h_ref, wl_ref, gq_ref, gkv_ref, wuq_ref, wuk_ref, wuv_ref, freq_ref,
                qc_ref, ckv_ref, kr_ref, krb_ref, kn_ref, vm_ref, *, bm, period, offset):
    hb = h_ref[...].astype(BF16)
    row = pl.program_id(0) * bm + lax.broadcasted_iota(jnp.int32, (bm, LANE), 0)
    pos = (jnp.bitwise_and(row, period - 1) + offset).astype(F32)
    ang = pos * freq_ref[...]
    lane = lax.broadcasted_iota(jnp.int32, (bm, LANE), 1)
    half = ROPE_DIM // 2
    sin_a = jnp.sin(ang)
    cos_t = jnp.where(lane < ROPE_DIM, jnp.cos(ang), 0.0)
    nsin_lo = jnp.where(lane < half, -sin_a, 0.0)
    sin_hi = jnp.where((lane >= half) & (lane < ROPE_DIM), sin_a, 0.0)

    c_q = _rms_norm(_dot(hb, wl_ref[:, 0:Q_LORA]), gq_ref[...]).astype(BF16)
    q_all = _dot(c_q, wuq_ref[...])
    for hd in range(MLA_HEADS):
        base = 2 * LANE * hd
        qc_ref[:, base:base + LANE] = q_all[:, base:base + LANE].astype(BF16)
        qr = _rope_lanes(q_all[:, base + LANE:base + 2 * LANE], cos_t, nsin_lo, sin_hi)
        qc_ref[:, base + LANE:base + 2 * LANE] = qr.astype(BF16)

    c_kv = _rms_norm(_dot(hb, wl_ref[:, Q_LORA:Q_LORA + KV_LORA]), gkv_ref[...])
    ckv_ref[...] = c_kv
    c_kvb = c_kv.astype(BF16)
    kn_ref[...] = _dot(c_kvb, wuk_ref[...]).astype(BF16)
    vm_ref[...] = _dot(c_kvb, wuv_ref[...]).astype(BF16)

    k_r = _rope_lanes(_dot(hb, wl_ref[:, Q_LORA + KV_LORA:Q_LORA + KV_LORA + LANE]), cos_t, nsin_lo, sin_hi)
    kr_ref[...] = k_r[:, 0:ROPE_DIM]
    krb_ref[...] = k_r.astype(BF16)


def _lat_proj(h, w_lat, g_cq, g_ckv, w_uq, w_uk, w_uv, freq, bm, period, offset):
    m = h.shape[0]
    assert period & (period - 1) == 0
    row = lambda i: (i, 0)
    const = lambda i: (0, 0)
    lat_cols = Q_LORA + KV_LORA + LANE
    return pl.pallas_call(
        functools.partial(_lat_kernel, bm=bm, period=period, offset=offset),
        grid=(m // bm,),
        in_specs=[
            pl.BlockSpec((bm, D_MODEL), row),
            pl.BlockSpec((D_MODEL, lat_cols), const),
            pl.BlockSpec((1, Q_LORA), const),
            pl.BlockSpec((1, KV_LORA), const),
            pl.BlockSpec((Q_LORA, 2 * LANE * MLA_HEADS), const),
            pl.BlockSpec((KV_LORA, MLA_HEADS * NOPE_DIM), const),
            pl.BlockSpec((KV_LORA, MLA_WIDTH), const),
            pl.BlockSpec((1, LANE), const),
        ],
        out_specs=[
            pl.BlockSpec((bm, 2 * LANE * MLA_HEADS), row),
            pl.BlockSpec((bm, KV_LORA), row),
            pl.BlockSpec((bm, ROPE_DIM), row),
            pl.BlockSpec((bm, LANE), row),
            pl.BlockSpec((bm, MLA_HEADS * NOPE_DIM), row),
            pl.BlockSpec((bm, MLA_WIDTH), row),
        ],
        out_shape=[
            jax.ShapeDtypeStruct((m, 2 * LANE * MLA_HEADS), BF16),
            jax.ShapeDtypeStruct((m, KV_LORA), F32),
            jax.ShapeDtypeStruct((m, ROPE_DIM), F32),
            jax.ShapeDtypeStruct((m, LANE), BF16),
            jax.ShapeDtypeStruct((m, MLA_HEADS * NOPE_DIM), BF16),
            jax.ShapeDtypeStruct((m, MLA_WIDTH), BF16),
        ],
        compiler_params=_params(("parallel",)),
        name="lat_proj",
    )(h, w_lat, g_cq, g_ckv, w_uq, w_uk, w_uv, freq)


def _strict_upper(n):
    j = lax.broadcasted_iota(jnp.int32, (n, n), 0)
    s = lax.broadcasted_iota(jnp.int32, (n, n), 1)
    return jnp.where(j > s, 1.0, 0.0).astype(BF16)


def _sb_step(q, k, v, u_tri, run, acc, vis):
    z = _dot_nt(q, k) * SB_SCALE
    l1m = -(jnp.maximum(z, 0.0) + jnp.log(1.0 + jnp.exp(-jnp.abs(z))))
    if vis is not None:
        l1m = jnp.where(vis, l1m, 0.0)
    hi = l1m.astype(BF16)
    lo = (l1m - hi.astype(F32)).astype(BF16)
    tail = _dot(hi, u_tri) + _dot(lo, u_tri)
    a = jnp.exp(z + l1m + tail + run)
    if vis is not None:
        a = jnp.where(vis, a, 0.0)
    acc = acc + _dot(a.astype(BF16), v)
    run = run + jnp.sum(l1m, axis=-1, keepdims=True)
    return run, acc


def _sb_prompt_kernel(q_ref, k_ref, v_ref, o_ref, *, tile):
    i = pl.program_id(2)
    q = q_ref[...]
    u_tri = _strict_upper(tile)
    r = lax.broadcasted_iota(jnp.int32, (tile, tile), 0)
    c = lax.broadcasted_iota(jnp.int32, (tile, tile), 1)
    start = pl.multiple_of(i * tile, tile)
    run = jnp.zeros((tile, 1), F32)
    acc = jnp.zeros((tile, SB_HEAD_DIM), F32)
    run, acc = _sb_step(q, k_ref[pl.ds(start, tile), :], v_ref[pl.ds(start, tile), :], u_tri, run, acc, c < r)

    def live(run):
        return jnp.max(run) > -SB_EXP_UNDERFLOW

    def cond(carry):
        return (carry[0] < i) & carry[3]

    def body(carry):
        t, run, acc, _ = carry
        s0 = pl.multiple_of((i - 1 - t) * tile, tile)
        run, acc = _sb_step(q, k_ref[pl.ds(s0, tile), :], v_ref[pl.ds(s0, tile), :], u_tri, run, acc, None)
        return t + 1, run, acc, live(run)

    _, run, acc, _ = lax.while_loop(cond, body, (jnp.int32(0), run, acc, live(run)))
    o_ref[...] = acc.astype(BF16)


def _sb_prompt(q, k, v, batch, seq):
    nq = seq // Q_TILE
    qmap = lambda b, h, i: (b * nq + i, h)
    kvmap = lambda b, h, i: (b, h)
    return pl.pallas_call(
        functools.partial(_sb_prompt_kernel, tile=Q_TILE),
        grid=(batch, SB_HEADS, nq),
        in_specs=[
            pl.BlockSpec((Q_TILE, SB_HEAD_DIM), qmap),
            pl.BlockSpec((seq, SB_HEAD_DIM), kvmap),
            pl.BlockSpec((seq, SB_HEAD_DIM), kvmap),
        ],
        out_specs=pl.BlockSpec((Q_TILE, SB_HEAD_DIM), qmap),
        out_shape=jax.ShapeDtypeStruct((batch * seq, SB_WIDTH), BF16),
        compiler_params=_params(("parallel", "parallel", "arbitrary")),
        name="sb_prompt",
    )(q, k, v)


def _sb_sample_kernel(q_ref, kn_ref, vn_ref, kc_ref, vc_ref, o_ref, *, t_new, past, tile):
    u_tri = _strict_upper(tile)
    r = lax.broadcasted_iota(jnp.int32, (t_new, tile), 0)
    c = lax.broadcasted_iota(jnp.int32, (t_new, tile), 1)
    zpad = jnp.zeros((tile - t_new, SB_HEAD_DIM), BF16)
    for hd in range(SB_HEADS):
        cols = slice(hd * SB_HEAD_DIM, (hd + 1) * SB_HEAD_DIM)
        q = q_ref[:, cols]
        k_new = jnp.concatenate([kn_ref[:, cols], zpad], axis=0)
        v_new = jnp.concatenate([vn_ref[:, cols], zpad], axis=0)
        run = jnp.zeros((t_new, 1), F32)
        acc = jnp.zeros((t_new, SB_HEAD_DIM), F32)
        run, acc = _sb_step(q, k_new, v_new, u_tri, run, acc, c < r)
        for blk in reversed(range(past // tile)):
            rows = slice(blk * tile, (blk + 1) * tile)
            run, acc = _sb_step(q, kc_ref[0, rows, cols].astype(BF16), vc_ref[0, rows, cols].astype(BF16),
                                u_tri, run, acc, None)
        o_ref[:, cols] = acc.astype(BF16)


def _sb_sample(q, k_new, v_new, cache_k, cache_v, streams, t_new, past):
    row = lambda b: (b, 0)
    cmap = lambda b: (b, 0, 0)
    return pl.pallas_call(
        functools.partial(_sb_sample_kernel, t_new=t_new, past=past, tile=LANE),
        grid=(streams,),
        in_specs=[
            pl.BlockSpec((t_new, SB_WIDTH), row),
            pl.BlockSpec((t_new, SB_WIDTH), row),
            pl.BlockSpec((t_new, SB_WIDTH), row),
            pl.BlockSpec((1, past, SB_WIDTH), cmap),
            pl.BlockSpec((1, past, SB_WIDTH), cmap),
        ],
        out_specs=pl.BlockSpec((t_new, SB_WIDTH), row),
        out_shape=jax.ShapeDtypeStruct((streams * t_new, SB_WIDTH), BF16),
        compiler_params=_params(("parallel",)),
        name="sb_sample",
    )(q, k_new, v_new, cache_k, cache_v)


def _softmax_step(q, kc, v, m, l, acc, vis):
    s = _dot_nt(q, kc) * MLA_SCALE
    if vis is not None:
        s = jnp.where(vis, s, -jnp.inf)
    m_new = jnp.maximum(m, jnp.max(s, axis=-1, keepdims=True))
    corr = jnp.exp(m - m_new)
    p = jnp.exp(s - m_new)
    l = l * corr + jnp.sum(p, axis=-1, keepdims=True)
    acc = acc * corr + _dot(p.astype(BF16), v)
    return m_new, l, acc


def _mla_prompt_kernel(q_ref, kn_ref, kr_ref, v_ref, o_ref, *, tile):
    i = pl.program_id(2)
    q = q_ref[...]

    def keys(s0):
        return jnp.concatenate([kn_ref[pl.ds(s0, tile), :], kr_ref[pl.ds(s0, tile), :]], axis=1)

    def body(t, carry):
        s0 = pl.multiple_of(t * tile, tile)
        return _softmax_step(q, keys(s0), v_ref[pl.ds(s0, tile), :], *carry, None)

    init = (jnp.full((tile, 1), -jnp.inf, F32), jnp.zeros((tile, 1), F32), jnp.zeros((tile, V_DIM), F32))
    carry = lax.fori_loop(0, i, body, init)
    r = lax.broadcasted_iota(jnp.int32, (tile, tile), 0)
    c = lax.broadcasted_iota(jnp.int32, (tile, tile), 1)
    vis = (c // CHUNK) <= (r // CHUNK)
    start = pl.multiple_of(i * tile, tile)
    m, l, acc = _softmax_step(q, keys(start), v_ref[pl.ds(start, tile), :], *carry, vis)
    o_ref[...] = (acc / l).astype(BF16)


def _mla_prompt(q_cat, k_nope, kr_pad, v, batch, seq):
    nq = seq // MLA_TILE
    qmap = lambda b, h, i: (b * nq + i, h)
    kvmap = lambda b, h, i: (b, h)
    return pl.pallas_call(
        functools.partial(_mla_prompt_kernel, tile=MLA_TILE),
        grid=(batch, MLA_HEADS, nq),
        in_specs=[
            pl.BlockSpec((MLA_TILE, 2 * LANE), qmap),
            pl.BlockSpec((seq, NOPE_DIM), kvmap),
            pl.BlockSpec((seq, LANE), lambda b, h, i: (b, 0)),
            pl.BlockSpec((seq, V_DIM), kvmap),
        ],
        out_specs=pl.BlockSpec((MLA_TILE, V_DIM), qmap),
        out_shape=jax.ShapeDtypeStruct((batch * seq, MLA_WIDTH), BF16),
        compiler_params=_params(("parallel", "parallel", "arbitrary")),
        name="mla_prompt",
    )(q_cat, k_nope, kr_pad, v)


def _mla_sample_kernel(q_ref, knn_ref, krn_ref, vn_ref, ckv_ref, krc_ref, wuk_ref, wuv_ref, o_ref, *, t_new):
    ckv = ckv_ref[0].astype(BF16)
    k_past = _dot(ckv, wuk_ref[...]).astype(BF16)
    v_past = _dot(ckv, wuv_ref[...]).astype(BF16)
    kr_past = krc_ref[0].astype(BF16)
    kr_new = krn_ref[...]
    for hd in range(MLA_HEADS):
        cols = slice(hd * NOPE_DIM, (hd + 1) * NOPE_DIM)
        q = q_ref[:, 2 * LANE * hd:2 * LANE * (hd + 1)]
        q_nope = q[:, 0:NOPE_DIM]
        q_rope = q[:, NOPE_DIM:NOPE_DIM + ROPE_DIM]
        s_p = (_dot_nt(q_nope, k_past[:, cols]) + _dot_nt(q_rope, kr_past)) * MLA_SCALE
        s_n = _dot_nt(q, jnp.concatenate([knn_ref[:, cols], kr_new], axis=1)) * MLA_SCALE
        m = jnp.maximum(jnp.max(s_p, axis=-1, keepdims=True), jnp.max(s_n, axis=-1, keepdims=True))
        p_p = jnp.exp(s_p - m)
        p_n = jnp.exp(s_n - m)
        l = jnp.sum(p_p, axis=-1, keepdims=True) + jnp.sum(p_n, axis=-1, keepdims=True)
        acc = _dot(p_p.astype(BF16), v_past[:, cols]) + _dot(p_n.astype(BF16), vn_ref[:, cols])
        o_ref[:, cols] = (acc / l).astype(BF16)


def _mla_sample(q_cat, kn_new, kr_new, v_new, cache_ckv, cache_kr, w_uk, w_uv, streams, t_new, past):
    row = lambda b: (b, 0)
    cmap = lambda b: (b, 0, 0)
    const = lambda b: (0, 0)
    return pl.pallas_call(
        functools.partial(_mla_sample_kernel, t_new=t_new),
        grid=(streams,),
        in_specs=[
            pl.BlockSpec((t_new, 2 * LANE * MLA_HEADS), row),
            pl.BlockSpec((t_new, MLA_HEADS * NOPE_DIM), row),
            pl.BlockSpec((t_new, LANE), row),
            pl.BlockSpec((t_new, MLA_WIDTH), row),
            pl.BlockSpec((1, past, KV_LORA), cmap),
            pl.BlockSpec((1, past, ROPE_DIM), cmap),
            pl.BlockSpec((KV_LORA, MLA_HEADS * NOPE_DIM), const),
            pl.BlockSpec((KV_LORA, MLA_WIDTH), const),
        ],
        out_specs=pl.BlockSpec((t_new, MLA_WIDTH), row),
        out_shape=jax.ShapeDtypeStruct((streams * t_new, MLA_WIDTH), BF16),
        compiler_params=_params(("parallel",)),
        name="mla_sample",
    )(q_cat, kn_new, kr_new, v_new, cache_ckv, cache_kr, w_uk, w_uv)


def _merge_ln_kernel(h_ref, osb_ref, omla_ref, wgs_ref, wgm_ref, bgs_ref, bgm_ref, wbs_ref, wbm_ref,
                     wo_ref, g_ref, b_ref, o_ref, hb_ref, mrg_ref):
    c = pl.program_id(1)
    nc = mrg_ref.shape[0]

    @pl.when(c == 0)
    def _():
        hb_ref[...] = h_ref[...].astype(BF16)

    hb = hb_ref[...]
    g_sb = jax.nn.sigmoid(_dot(hb, wgs_ref[...]) + bgs_ref[...])
    g_mla = jax.nn.sigmoid(_dot(hb, wgm_ref[...]) + bgm_ref[...])
    merged = g_sb * _dot(osb_ref[...], wbs_ref[...]) + g_mla * _dot(omla_ref[...], wbm_ref[...])
    mrg_ref[c] = merged.astype(BF16)

    @pl.when(c == nc - 1)
    def _():
        mix = _dot(mrg_ref[0], wo_ref[0:MERGE_CHUNK, :])
        for cc in range(1, nc):
            mix = mix + _dot(mrg_ref[cc], wo_ref[cc * MERGE_CHUNK:(cc + 1) * MERGE_CHUNK, :])
        o_ref[...] = _layer_norm(ALPHA * h_ref[...] + mix, g_ref[...], b_ref[...])


def _merge_ln(h, o_sb, o_mla, w_gs, w_gm, b_gs, b_gm, w_bs, w_bm, w_o, g, b, bm):
    m = h.shape[0]
    nc = D_MODEL // MERGE_CHUNK
    row = lambda i, c: (i, 0)
    col = lambda i, c: (0, c)
    const = lambda i, c: (0, 0)
    return pl.pallas_call(
        _merge_ln_kernel,
        grid=(m // bm, nc),
        in_specs=[
            pl.BlockSpec((bm, D_MODEL), row),
            pl.BlockSpec((bm, SB_WIDTH), row),
            pl.BlockSpec((bm, MLA_WIDTH), row),
            pl.BlockSpec((D_MODEL, MERGE_CHUNK), col),
            pl.BlockSpec((D_MODEL, MERGE_CHUNK), col),
            pl.BlockSpec((1, MERGE_CHUNK), col),
            pl.BlockSpec((1, MERGE_CHUNK), col),
            pl.BlockSpec((SB_WIDTH, MERGE_CHUNK), col),
            pl.BlockSpec((MLA_WIDTH, MERGE_CHUNK), col),
            pl.BlockSpec((D_MODEL, D_MODEL), const),
            pl.BlockSpec((1, D_MODEL), const),
            pl.BlockSpec((1, D_MODEL), const),
        ],
        out_specs=pl.BlockSpec((bm, D_MODEL), row),
        out_shape=jax.ShapeDtypeStruct((m, D_MODEL), F32),
        scratch_shapes=[pltpu.VMEM((bm, D_MODEL), BF16), pltpu.VMEM((nc, bm, MERGE_CHUNK), BF16)],
        compiler_params=_params(("parallel", "arbitrary")),
        name="merge_ln",
    )(h, o_sb, o_mla, w_gs, w_gm, b_gs, b_gm, w_bs, w_bm, w_o, g, b)


def _prep_weights(ffn1_w_in, ffn1_w_out, w_in, b_gate, w_uq, w_ukv, w_br_sb, w_br_mla, w_o, ffn2_w_in, ffn2_w_out):
    def ffn(w_i, w_out):
        pad = D_FF_PAD - D_FF
        wg = jnp.pad(w_i[:, :D_FF].astype(BF16), ((0, 0), (0, pad)))
        wu = jnp.pad(w_i[:, D_FF:].astype(BF16), ((0, 0), (0, pad)))
        wo = jnp.pad(w_out.astype(BF16), ((0, pad), (0, 0)))
        return wg, wu, wo

    c0 = 3 * SB_WIDTH
    c1 = c0 + Q_LORA + KV_LORA + ROPE_DIM
    w_qkv = w_in[:, :c0].astype(BF16)
    w_lat = jnp.pad(w_in[:, c0:c1].astype(BF16), ((0, 0), (0, LANE - ROPE_DIM)))
    w_gs = w_in[:, c1:c1 + D_MODEL].astype(BF16)
    w_gm = w_in[:, c1 + D_MODEL:].astype(BF16)
    b_gs = b_gate[:D_MODEL].reshape(1, D_MODEL)
    b_gm = b_gate[D_MODEL:].reshape(1, D_MODEL)
    w_uq3 = w_uq.astype(BF16).reshape(Q_LORA, MLA_HEADS, NOPE_DIM + ROPE_DIM)
    w_uq_p = jnp.pad(w_uq3, ((0, 0), (0, 0), (0, 2 * LANE - NOPE_DIM - ROPE_DIM))).reshape(Q_LORA, 2 * LANE * MLA_HEADS)
    w_ukv3 = w_ukv.astype(BF16).reshape(KV_LORA, MLA_HEADS, NOPE_DIM + V_DIM)
    w_uk = w_ukv3[:, :, :NOPE_DIM].reshape(KV_LORA, MLA_HEADS * NOPE_DIM)
    w_uv = w_ukv3[:, :, NOPE_DIM:].reshape(KV_LORA, MLA_WIDTH)
    return dict(ffn1=ffn(ffn1_w_in, ffn1_w_out), ffn2=ffn(ffn2_w_in, ffn2_w_out),
                w_qkv=w_qkv, w_lat=w_lat, w_gs=w_gs, w_gm=w_gm, b_gs=b_gs, b_gm=b_gm,
                w_uq=w_uq_p, w_uk=w_uk, w_uv=w_uv,
                w_bs=w_br_sb.astype(BF16), w_bm=w_br_mla.astype(BF16), w_o=w_o.astype(BF16))


def _rope_freq():
    inv_freq = ROPE_THETA ** (-jnp.arange(0, ROPE_DIM, 2, dtype=F32) / ROPE_DIM)
    return jnp.concatenate([inv_freq, inv_freq, jnp.zeros((LANE - ROPE_DIM,), F32)]).reshape(1, LANE)


def _row2(v):
    return v.reshape(1, -1)


def kernel(x_prompt, x_sample, cache_sb_k, cache_sb_v, cache_mla_ckv, cache_mla_krope, ffn1_w_in, ffn1_w_out, ln1_g, ln1_b, w_in, b_gate, g_cq, w_uq, g_ckv, w_ukv, w_br_sb, w_br_mla, w_o, ln2_g, ln2_b, ffn2_w_in, ffn2_w_out, ln3_g, ln3_b):
    assert ffn1_w_in.shape[0] == DEPTH == 1
    b_p, t_p, _ = x_prompt.shape
    b_s, t_s, _ = x_sample.shape
    past = cache_sb_k.shape[2]
    w = _prep_weights(ffn1_w_in[0], ffn1_w_out[0], w_in[0], b_gate[0], w_uq[0], w_ukv[0],
                      w_br_sb[0], w_br_mla[0], w_o[0], ffn2_w_in[0], ffn2_w_out[0])
    freq = _rope_freq()
    ln1 = (_row2(ln1_g[0]), _row2(ln1_b[0]))
    ln2 = (_row2(ln2_g[0]), _row2(ln2_b[0]))
    ln3 = (_row2(ln3_g[0]), _row2(ln3_b[0]))
    g_cq2, g_ckv2 = _row2(g_cq[0]), _row2(g_ckv[0])

    def rowwise_front(x2, bm, period, offset):
        h1 = _ffn_ln(x2, *w["ffn1"], *ln1, bm)
        q, k, v, kb, vb = _qkv_proj(h1, w["w_qkv"], min(bm, ROW_TILE_SMALL))
        qc, ckv, kr, krb, kn, vm = _lat_proj(h1, w["w_lat"], g_cq2, g_ckv2, w["w_uq"], w["w_uk"], w["w_uv"],
                                             freq, bm, period, offset)
        return h1, (q, kb, vb), (qc, kn, krb, vm), (k, v, ckv, kr)

    def rowwise_back(h1, o_sb, o_mla, bm):
        h2 = _merge_ln(h1, o_sb, o_mla, w["w_gs"], w["w_gm"], w["b_gs"], w["b_gm"], w["w_bs"], w["w_bm"],
                       w["w_o"], *ln2, min(bm, ROW_TILE_SMALL))
        return _ffn_ln(h2, *w["ffn2"], *ln3, bm)

    m_p = b_p * t_p
    h1p, (q, kb, vb), (qc, kn, krb, vm), rows_p = rowwise_front(x_prompt.reshape(m_p, D_MODEL), ROW_TILE, t_p, 0)
    o_sb = _sb_prompt(q, kb, vb, b_p, t_p)
    o_mla = _mla_prompt(qc, kn, krb, vm, b_p, t_p)
    y_p = rowwise_back(h1p, o_sb, o_mla, ROW_TILE)

    m_s = b_s * t_s
    h1s, (q, kb, vb), (qc, kn, krb, vm), rows_s = rowwise_front(x_sample.reshape(m_s, D_MODEL), m_s, t_s, past)
    o_sb = _sb_sample(q, kb, vb, cache_sb_k[0].reshape(b_s, past, SB_WIDTH),
                      cache_sb_v[0].reshape(b_s, past, SB_WIDTH), b_s, t_s, past)
    o_mla = _mla_sample(qc, kn, krb, vm, cache_mla_ckv[0], cache_mla_krope[0], w["w_uk"], w["w_uv"], b_s, t_s, past)
    y_s = rowwise_back(h1s, o_sb, o_mla, m_s)

    def cache_rows(rows, b, t):
        k, v, ckv, kr = rows
        return (k.reshape(1, b, t, SB_HEADS, SB_HEAD_DIM), v.reshape(1, b, t, SB_HEADS, SB_HEAD_DIM),
                ckv.reshape(1, b, t, KV_LORA), kr.reshape(1, b, t, ROPE_DIM))

    return (y_p.reshape(b_p, t_p, D_MODEL), y_s.reshape(b_s, t_s, D_MODEL),
            *cache_rows(rows_p, b_p, t_p), *cache_rows(rows_s, b_s, t_s))
```

```python
import functools

import jax
import jax.numpy as jnp
from jax import lax
from jax.experimental import pallas as pl
from jax.experimental.pallas import tpu as pltpu

D_MODEL = 2048
DEPTH = 1
CHUNK = 64
SB_HEADS = 8
SB_HEAD_DIM = 128
SB_WIDTH = SB_HEADS * SB_HEAD_DIM
MLA_HEADS = 8
Q_LORA = 512
KV_LORA = 512
NOPE_DIM = 128
ROPE_DIM = 64
V_DIM = 128
MLA_WIDTH = MLA_HEADS * V_DIM
ROPE_THETA = 10000.0
D_FF = 5504
LN_EPS = 1e-5
RMS_EPS = 1e-6
ALPHA = (2 * DEPTH) ** 0.25
SB_SCALE = SB_HEAD_DIM ** -0.5
MLA_SCALE = (NOPE_DIM + ROPE_DIM) ** -0.5

LANE = 128
FF_CHUNK = 512
FF_STEPS = -(-D_FF // FF_CHUNK)
FF_OVERLAP = FF_STEPS * FF_CHUNK - D_FF
MERGE_CHUNK = 512
Q_TILE = 256
MLA_TILE = 512
SB_EXP_UNDERFLOW = 120.0
ROW_TILE_FFN = 1024
ROW_TILE = 512
ROW_TILE_SMALL = 256
VMEM_LIMIT = 56 * 1024 * 1024

BF16 = jnp.bfloat16
F32 = jnp.float32


def _params(sem):
    return pltpu.CompilerParams(dimension_semantics=sem, vmem_limit_bytes=VMEM_LIMIT)


def _dot(a, b):
    return jnp.dot(a, b, preferred_element_type=F32)


def _dot_nt(a, b):
    return lax.dot_general(a, b, (((1,), (1,)), ((), ())), preferred_element_type=F32)


def _layer_norm(r, g, b):
    mu = jnp.mean(r, axis=-1, keepdims=True)
    d = r - mu
    var = jnp.mean(d * d, axis=-1, keepdims=True)
    return d * lax.rsqrt(var + LN_EPS) * g + b


def _rms_norm(x, g):
    return x * lax.rsqrt(jnp.mean(x * x, axis=-1, keepdims=True) + RMS_EPS) * g


def _ffn_ln_kernel(x_ref, wg_ref, wu_ref, wo_ref, g_ref, b_ref, o_ref, xb_ref):
    j = pl.program_id(1)
    last = pl.num_programs(1) - 1

    @pl.when(j == 0)
    def _():
        xb_ref[...] = x_ref[...].astype(BF16)
        o_ref[...] = jnp.zeros_like(o_ref)

    xb = xb_ref[...]
    gate = _dot(xb, wg_ref[...])
    up = _dot(xb, wu_ref[...])
    act = gate * jax.nn.sigmoid(gate) * up
    col = lax.broadcasted_iota(jnp.int32, act.shape, 1)
    act = jnp.where((j < last) | (col >= FF_OVERLAP), act, 0.0).astype(BF16)
    o_ref[...] += _dot(act, wo_ref[...])

    @pl.when(j == last)
    def _():
        r = ALPHA * x_ref[...] + 0.5 * o_ref[...]
        o_ref[...] = _layer_norm(r, g_ref[...], b_ref[...])


def _ffn_ln(x, w_in, w_out, g, b, bm):
    m = x.shape[0]
    per = FF_CHUNK // LANE
    chunk_start = lambda j, base=0: (base // LANE + j * per - (j // (FF_STEPS - 1)) * (FF_OVERLAP // LANE)) * LANE
    return pl.pallas_call(
        _ffn_ln_kernel,
        grid=(m // bm, FF_STEPS),
        in_specs=[
            pl.BlockSpec((bm, D_MODEL), lambda i, j: (i, 0), pipeline_mode=pl.Buffered(1)),
            pl.BlockSpec((pl.Element(D_MODEL), pl.Element(FF_CHUNK)), lambda i, j: (0, chunk_start(j))),
            pl.BlockSpec((pl.Element(D_MODEL), pl.Element(FF_CHUNK)), lambda i, j: (0, chunk_start(j, D_FF))),
            pl.BlockSpec((pl.Element(FF_CHUNK), pl.Element(D_MODEL)), lambda i, j: (chunk_start(j), 0)),
            pl.BlockSpec((1, D_MODEL), lambda i, j: (0, 0)),
            pl.BlockSpec((1, D_MODEL), lambda i, j: (0, 0)),
        ],
        out_specs=pl.BlockSpec((bm, D_MODEL), lambda i, j: (i, 0)),
        out_shape=jax.ShapeDtypeStruct((m, D_MODEL), F32),
        scratch_shapes=[pltpu.VMEM((bm, D_MODEL), BF16)],
        compiler_params=_params(("parallel", "arbitrary")),
        name="ffn_ln",
    )(x, w_in, w_in, w_out, g, b)


def _qkv_kernel(h_ref, w_ref, q_ref, k_ref, v_ref, kb_ref, vb_ref):
    hb = h_ref[...].astype(BF16)
    q_ref[...] = _dot(hb, w_ref[:, 0:SB_WIDTH]).astype(BF16)
    k = _dot(hb, w_ref[:, SB_WIDTH:2 * SB_WIDTH])
    k_ref[...] = k
    kb_ref[...] = k.astype(BF16)
    v = _dot(hb, w_ref[:, 2 * SB_WIDTH:3 * SB_WIDTH])
    v_ref[...] = v
    vb_ref[...] = v.astype(BF16)


def _qkv_proj(h, w_qkv, bm):
    m = h.shape[0]
    row = lambda i: (i, 0)
    const = lambda i: (0, 0)
    return pl.pallas_call(
        _qkv_kernel,
        grid=(m // bm,),
        in_specs=[pl.BlockSpec((bm, D_MODEL), row), pl.BlockSpec((D_MODEL, 3 * SB_WIDTH), const)],
        out_specs=[pl.BlockSpec((bm, SB_WIDTH), row)] * 5,
        out_shape=[
            jax.ShapeDtypeStruct((m, SB_WIDTH), BF16),
            jax.ShapeDtypeStruct((m, SB_WIDTH), F32),
            jax.ShapeDtypeStruct((m, SB_WIDTH), F32),
            jax.ShapeDtypeStruct((m, SB_WIDTH), BF16),
            jax.ShapeDtypeStruct((m, SB_WIDTH), BF16),
        ],
        compiler_params=_params(("parallel",)),
        name="qkv_proj",
    )(h, w_qkv)


def _rope_lanes(x, cos_t, nsin_lo, sin_hi):
    return x * cos_t + pltpu.roll(x, LANE - ROPE_DIM // 2, 1) * nsin_lo + pltpu.roll(x, ROPE_DIM // 2, 1) * sin_hi


def _lat_kernel(h_ref, wl_ref, gq_ref, gkv_ref, wuq_ref, wuk_ref, wuv_ref, freq_ref,
                qc_ref, ckv_ref, kr_ref, krb_ref, kn_ref, vm_ref, *, bm, period, offset):
    hb = h_ref[...].astype(BF16)
    row = pl.program_id(0) * bm + lax.broadcasted_iota(jnp.int32, (bm, LANE), 0)
    pos = (jnp.bitwise_and(row, period - 1) + offset).astype(F32)
    ang = pos * freq_ref[...]
    lane = lax.broadcasted_iota(jnp.int32, (bm, LANE), 1)
    half = ROPE_DIM // 2
    sin_a = jnp.sin(ang)
    cos_t = jnp.where(lane < ROPE_DIM, jnp.cos(ang), 0.0)
    nsin_lo = jnp.where(lane < half, -sin_a, 0.0)
    sin_hi = jnp.where((lane >= half) & (lane < ROPE_DIM), sin_a, 0.0)

    c_q = _rms_norm(_dot(hb, wl_ref[:, 0:Q_LORA]), gq_ref[...]).astype(BF16)
    q_all = _dot(c_q, wuq_ref[...])
    for hd in range(MLA_HEADS):
        base = 2 * LANE * hd
        qc_ref[:, base:base + LANE] = q_all[:, base:base + LANE].astype(BF16)
        qr = _rope_lanes(q_all[:, base + LANE:base + 2 * LANE], cos_t, nsin_lo, sin_hi)
        qc_ref[:, base + LANE:base + 2 * LANE] = qr.astype(BF16)

    c_kv = _rms_norm(_dot(hb, wl_ref[:, Q_LORA:Q_LORA + KV_LORA]), gkv_ref[...])
    ckv_ref[...] = c_kv
    c_kvb = c_kv.astype(BF16)
    kn_ref[...] = _dot(c_kvb, wuk_ref[...]).astype(BF16)
    vm_ref[...] = _dot(c_kvb, wuv_ref[...]).astype(BF16)

    k_r = _rope_lanes(_dot(hb, wl_ref[:, Q_LORA + KV_LORA:Q_LORA + KV_LORA + LANE]), cos_t, nsin_lo, sin_hi)
    kr_ref[...] = k_r[:, 0:ROPE_DIM]
    krb_ref[...] = k_r.astype(BF16)


def _lat_proj(h, w_lat, g_cq, g_ckv, w_uq, w_uk, w_uv, freq, bm, period, offset):
    m = h.shape[0]
    assert period & (period - 1) == 0
    row = lambda i: (i, 0)
    const = lambda i: (0, 0)
    lat_cols = Q_LORA + KV_LORA + LANE
    return pl.pallas_call(
        functools.partial(_lat_kernel, bm=bm, period=period, offset=offset),
        grid=(m // bm,),
        in_specs=[
            pl.BlockSpec((bm, D_MODEL), row),
            pl.BlockSpec((D_MODEL, lat_cols), const),
            pl.BlockSpec((1, Q_LORA), const),
            pl.BlockSpec((1, KV_LORA), const),
            pl.BlockSpec((Q_LORA, 2 * LANE * MLA_HEADS), const),
            pl.BlockSpec((KV_LORA, MLA_HEADS * NOPE_DIM), const),
            pl.BlockSpec((KV_LORA, MLA_WIDTH), const),
            pl.BlockSpec((1, LANE), const),
        ],
        out_specs=[
            pl.BlockSpec((bm, 2 * LANE * MLA_HEADS), row),
            pl.BlockSpec((bm, KV_LORA), row),
            pl.BlockSpec((bm, ROPE_DIM), row),
            pl.BlockSpec((bm, LANE), row),
            pl.BlockSpec((bm, MLA_HEADS * NOPE_DIM), row),
            pl.BlockSpec((bm, MLA_WIDTH), row),
        ],
        out_shape=[
            jax.ShapeDtypeStruct((m, 2 * LANE * MLA_HEADS), BF16),
            jax.ShapeDtypeStruct((m, KV_LORA), F32),
            jax.ShapeDtypeStruct((m, ROPE_DIM), F32),
            jax.ShapeDtypeStruct((m, LANE), BF16),
            jax.ShapeDtypeStruct((m, MLA_HEADS * NOPE_DIM), BF16),
            jax.ShapeDtypeStruct((m, MLA_WIDTH), BF16),
        ],
        compiler_params=_params(("parallel",)),
        name="lat_proj",
    )(h, w_lat, g_cq, g_ckv, w_uq, w_uk, w_uv, freq)


def _strict_upper(n):
    j = lax.broadcasted_iota(jnp.int32, (n, n), 0)
    s = lax.broadcasted_iota(jnp.int32, (n, n), 1)
    return jnp.where(j > s, 1.0, 0.0).astype(BF16)


def _sb_step(q, k, v, u_tri, run, acc, vis):
    z = _dot_nt(q, k) * SB_SCALE
    l1m = -(jnp.maximum(z, 0.0) + jnp.log(1.0 + jnp.exp(-jnp.abs(z))))
    if vis is not None:
        l1m = jnp.where(vis, l1m, 0.0)
    hi = l1m.astype(BF16)
    lo = (l1m - hi.astype(F32)).astype(BF16)
    tail = _dot(hi, u_tri) + _dot(lo, u_tri)
    a = jnp.exp(z + l1m + tail + run)
    if vis is not None:
        a = jnp.where(vis, a, 0.0)
    acc = acc + _dot(a.astype(BF16), v)
    run = run + jnp.sum(l1m, axis=-1, keepdims=True)
    return run, acc


def _sb_prompt_kernel(q_ref, k_ref, v_ref, o_ref, *, tile):
    i = pl.program_id(2)
    q = q_ref[...]
    u_tri = _strict_upper(tile)
    r = lax.broadcasted_iota(jnp.int32, (tile, tile), 0)
    c = lax.broadcasted_iota(jnp.int32, (tile, tile), 1)
    start = pl.multiple_of(i * tile, tile)
    run = jnp.zeros((tile, 1), F32)
    acc = jnp.zeros((tile, SB_HEAD_DIM), F32)
    run, acc = _sb_step(q, k_ref[pl.ds(start, tile), :], v_ref[pl.ds(start, tile), :], u_tri, run, acc, c < r)

    def live(run):
        return jnp.max(run) > -SB_EXP_UNDERFLOW

    def cond(carry):
        return (carry[0] < i) & carry[3]

    def body(carry):
        t, run, acc, _ = carry
        s0 = pl.multiple_of((i - 1 - t) * tile, tile)
        run, acc = _sb_step(q, k_ref[pl.ds(s0, tile), :], v_ref[pl.ds(s0, tile), :], u_tri, run, acc, None)
        return t + 1, run, acc, live(run)

    _, run, acc, _ = lax.while_loop(cond, body, (jnp.int32(0), run, acc, live(run)))
    o_ref[...] = acc.astype(BF16)


def _sb_prompt(q, k, v, batch, seq):
    nq = seq // Q_TILE
    qmap = lambda b, h, i: (b * nq + i, h)
    kvmap = lambda b, h, i: (b, h)
    return pl.pallas_call(
        functools.partial(_sb_prompt_kernel, tile=Q_TILE),
        grid=(batch, SB_HEADS, nq),
        in_specs=[
            pl.BlockSpec((Q_TILE, SB_HEAD_DIM), qmap),
            pl.BlockSpec((seq, SB_HEAD_DIM), kvmap),
            pl.BlockSpec((seq, SB_HEAD_DIM), kvmap),
        ],
        out_specs=pl.BlockSpec((Q_TILE, SB_HEAD_DIM), qmap),
        out_shape=jax.ShapeDtypeStruct((batch * seq, SB_WIDTH), BF16),
        compiler_params=_params(("parallel", "parallel", "arbitrary")),
        name="sb_prompt",
    )(q, k, v)


def _sb_sample_kernel(q_ref, kn_ref, vn_ref, kc_ref, vc_ref, o_ref, sfx_ref, *, t_new, past):
    n_keys = past + LANE
    n_lane = SB_HEADS * t_new

    @pl.when(pl.program_id(0) == 0)
    def _():
        s = lax.broadcasted_iota(jnp.int32, (n_keys, n_keys), 0)
        j = lax.broadcasted_iota(jnp.int32, (n_keys, n_keys), 1)
        sfx_ref[...] = jnp.where(j > s, 1.0, 0.0).astype(BF16)

    def all_heads(cache_ref, new_ref):
        cached = jnp.concatenate(
            [cache_ref[0, pl.ds(hd, past, stride=SB_HEADS), :] for hd in range(SB_HEADS)], axis=1)
        pad = jnp.zeros((LANE - t_new, SB_WIDTH), BF16)
        return jnp.concatenate([cached.astype(BF16), new_ref[...], pad], axis=0)

    q_rep = jnp.concatenate([q_ref[...]] * SB_HEADS, axis=0)
    row_head = lax.broadcasted_iota(jnp.int32, q_rep.shape, 0) // t_new
    col_head = lax.broadcasted_iota(jnp.int32, q_rep.shape, 1) // SB_HEAD_DIM
    q_bd = jnp.where(row_head == col_head, q_rep, jnp.zeros_like(q_rep))

    z = _dot_nt(all_heads(kc_ref, kn_ref), q_bd) * SB_SCALE
    key = lax.broadcasted_iota(jnp.int32, z.shape, 0)
    query = lax.broadcasted_iota(jnp.int32, z.shape, 1) % t_new
    vis = key < past + query
    l1m = jnp.where(vis, -(jnp.maximum(z, 0.0) + jnp.log(1.0 + jnp.exp(-jnp.abs(z)))), 0.0)
    hi = l1m.astype(BF16)
    lo = (l1m - hi.astype(F32)).astype(BF16)
    tail = _dot(sfx_ref[...], hi) + _dot(sfx_ref[...], lo)
    a = jnp.where(vis, jnp.exp(z + l1m + tail), 0.0)
    out = _dot(a.T.astype(BF16), all_heads(vc_ref, vn_ref))
    for hd in range(SB_HEADS):
        cols = slice(hd * SB_HEAD_DIM, (hd + 1) * SB_HEAD_DIM)
        o_ref[:, cols] = out[hd * t_new:(hd + 1) * t_new, cols].astype(BF16)


def _sb_sample(q, k_new, v_new, cache_k, cache_v, streams, t_new, past):
    assert SB_HEADS * t_new == LANE and t_new % 16 == 0
    n_keys = past + LANE
    row = lambda b: (b, 0)
    cmap = lambda b: (b, 0, 0)
    return pl.pallas_call(
        functools.partial(_sb_sample_kernel, t_new=t_new, past=past),
        grid=(streams,),
        in_specs=[
            pl.BlockSpec((t_new, SB_WIDTH), row),
            pl.BlockSpec((t_new, SB_WIDTH), row),
            pl.BlockSpec((t_new, SB_WIDTH), row),
            pl.BlockSpec((1, past * SB_HEADS, SB_HEAD_DIM), cmap),
            pl.BlockSpec((1, past * SB_HEADS, SB_HEAD_DIM), cmap),
        ],
        out_specs=pl.BlockSpec((t_new, SB_WIDTH), row),
        out_shape=jax.ShapeDtypeStruct((streams * t_new, SB_WIDTH), BF16),
        scratch_shapes=[pltpu.VMEM((n_keys, n_keys), BF16)],
        compiler_params=_params(("arbitrary",)),
        name="sb_sample",
    )(q, k_new, v_new, cache_k, cache_v)


def _softmax_step(q, kc, v, m, l, acc, vis):
    s = _dot_nt(q, kc) * MLA_SCALE
    if vis is not None:
        s = jnp.where(vis, s, -jnp.inf)
    m_new = jnp.maximum(m, jnp.max(s, axis=-1, keepdims=True))
    corr = jnp.exp(m - m_new)
    p = jnp.exp(s - m_new)
    l = l * corr + jnp.sum(p, axis=-1, keepdims=True)
    acc = acc * corr + _dot(p.astype(BF16), v)
    return m_new, l, acc


def _mla_prompt_kernel(q_ref, kn_ref, kr_ref, v_ref, o_ref, *, tile):
    i = pl.program_id(2)
    q = q_ref[...]

    def keys(s0):
        return jnp.concatenate([kn_ref[pl.ds(s0, tile), :], kr_ref[pl.ds(s0, tile), :]], axis=1)

    def body(t, carry):
        s0 = pl.multiple_of(t * tile, tile)
        return _softmax_step(q, keys(s0), v_ref[pl.ds(s0, tile), :], *carry, None)

    init = (jnp.full((tile, 1), -jnp.inf, F32), jnp.zeros((tile, 1), F32), jnp.zeros((tile, V_DIM), F32))
    carry = lax.fori_loop(0, i, body, init)
    r = lax.broadcasted_iota(jnp.int32, (tile, tile), 0)
    c = lax.broadcasted_iota(jnp.int32, (tile, tile), 1)
    vis = (c // CHUNK) <= (r // CHUNK)
    start = pl.multiple_of(i * tile, tile)
    m, l, acc = _softmax_step(q, keys(start), v_ref[pl.ds(start, tile), :], *carry, vis)
    o_ref[...] = (acc / l).astype(BF16)


def _mla_prompt(q_cat, k_nope, kr_pad, v, batch, seq):
    nq = seq // MLA_TILE
    qmap = lambda b, h, i: (b * nq + i, h)
    kvmap = lambda b, h, i: (b, h)
    return pl.pallas_call(
        functools.partial(_mla_prompt_kernel, tile=MLA_TILE),
        grid=(batch, MLA_HEADS, nq),
        in_specs=[
            pl.BlockSpec((MLA_TILE, 2 * LANE), qmap),
            pl.BlockSpec((seq, NOPE_DIM), kvmap),
            pl.BlockSpec((seq, LANE), lambda b, h, i: (b, 0)),
            pl.BlockSpec((seq, V_DIM), kvmap),
        ],
        out_specs=pl.BlockSpec((MLA_TILE, V_DIM), qmap),
        out_shape=jax.ShapeDtypeStruct((batch * seq, MLA_WIDTH), BF16),
        compiler_params=_params(("parallel", "parallel", "arbitrary")),
        name="mla_prompt",
    )(q_cat, k_nope, kr_pad, v)


def _mla_sample_kernel(q_ref, knn_ref, krn_ref, vn_ref, ckv_ref, krc_ref, wuk_ref, wuv_ref, o_ref, *, t_new):
    ckv = ckv_ref[0].astype(BF16)
    k_past = _dot(ckv, wuk_ref[...]).astype(BF16)
    v_past = _dot(ckv, wuv_ref[...]).astype(BF16)
    kr_past = krc_ref[0].astype(BF16)
    kr_new = krn_ref[...]
    for hd in range(MLA_HEADS):
        cols = slice(hd * NOPE_DIM, (hd + 1) * NOPE_DIM)
        q = q_ref[:, 2 * LANE * hd:2 * LANE * (hd + 1)]
        q_nope = q[:, 0:NOPE_DIM]
        q_rope = q[:, NOPE_DIM:NOPE_DIM + ROPE_DIM]
        s_p = (_dot_nt(q_nope, k_past[:, cols]) + _dot_nt(q_rope, kr_past)) * MLA_SCALE
        s_n = _dot_nt(q, jnp.concatenate([knn_ref[:, cols], kr_new], axis=1)) * MLA_SCALE
        m = jnp.maximum(jnp.max(s_p, axis=-1, keepdims=True), jnp.max(s_n, axis=-1, keepdims=True))
        p_p = jnp.exp(s_p - m)
        p_n = jnp.exp(s_n - m)
        l = jnp.sum(p_p, axis=-1, keepdims=True) + jnp.sum(p_n, axis=-1, keepdims=True)
        acc = _dot(p_p.astype(BF16), v_past[:, cols]) + _dot(p_n.astype(BF16), vn_ref[:, cols])
        o_ref[:, cols] = (acc / l).astype(BF16)


def _mla_sample(q_cat, kn_new, kr_new, v_new, cache_ckv, cache_kr, w_uk, w_uv, streams, t_new, past):
    row = lambda b: (b, 0)
    cmap = lambda b: (b, 0, 0)
    const = lambda b: (0, 0)
    return pl.pallas_call(
        functools.partial(_mla_sample_kernel, t_new=t_new),
        grid=(streams,),
        in_specs=[
            pl.BlockSpec((t_new, 2 * LANE * MLA_HEADS), row),
            pl.BlockSpec((t_new, MLA_HEADS * NOPE_DIM), row),
            pl.BlockSpec((t_new, LANE), row),
            pl.BlockSpec((t_new, MLA_WIDTH), row),
            pl.BlockSpec((1, past, KV_LORA), cmap),
            pl.BlockSpec((1, past, ROPE_DIM), cmap),
            pl.BlockSpec((KV_LORA, MLA_HEADS * NOPE_DIM), const),
            pl.BlockSpec((KV_LORA, MLA_WIDTH), const),
        ],
        out_specs=pl.BlockSpec((t_new, MLA_WIDTH), row),
        out_shape=jax.ShapeDtypeStruct((streams * t_new, MLA_WIDTH), BF16),
        compiler_params=_params(("parallel",)),
        name="mla_sample",
    )(q_cat, kn_new, kr_new, v_new, cache_ckv, cache_kr, w_uk, w_uv)


def _merge_ln_kernel(h_ref, osb_ref, omla_ref, wgs_ref, wgm_ref, bgs_ref, bgm_ref, wbs_ref, wbm_ref,
                     wo_ref, g_ref, b_ref, o_ref, hb_ref):
    c = pl.program_id(1)
    last = pl.num_programs(1) - 1

    @pl.when(c == 0)
    def _():
        hb_ref[...] = h_ref[...].astype(BF16)
        o_ref[...] = jnp.zeros_like(o_ref)

    hb = hb_ref[...]
    g_sb = jax.nn.sigmoid(_dot(hb, wgs_ref[...]) + bgs_ref[...])
    g_mla = jax.nn.sigmoid(_dot(hb, wgm_ref[...]) + bgm_ref[...])
    merged = g_sb * _dot(osb_ref[...], wbs_ref[...]) + g_mla * _dot(omla_ref[...], wbm_ref[...])
    o_ref[...] += _dot(merged.astype(BF16), wo_ref[...])

    @pl.when(c == last)
    def _():
        o_ref[...] = _layer_norm(ALPHA * h_ref[...] + o_ref[...], g_ref[...], b_ref[...])


def _merge_ln(h, o_sb, o_mla, w_gs, w_gm, b_gs, b_gm, w_bs, w_bm, w_o, g, b, bm):
    m = h.shape[0]
    nc = D_MODEL // MERGE_CHUNK
    row = lambda i, c: (i, 0)
    col = lambda i, c: (0, c)
    const = lambda i, c: (0, 0)
    return pl.pallas_call(
        _merge_ln_kernel,
        grid=(m // bm, nc),
        in_specs=[
            pl.BlockSpec((bm, D_MODEL), row),
            pl.BlockSpec((bm, SB_WIDTH), row),
            pl.BlockSpec((bm, MLA_WIDTH), row),
            pl.BlockSpec((D_MODEL, MERGE_CHUNK), col),
            pl.BlockSpec((D_MODEL, MERGE_CHUNK), col),
            pl.BlockSpec((1, MERGE_CHUNK), col),
            pl.BlockSpec((1, MERGE_CHUNK), col),
            pl.BlockSpec((SB_WIDTH, MERGE_CHUNK), col),
            pl.BlockSpec((MLA_WIDTH, MERGE_CHUNK), col),
            pl.BlockSpec((MERGE_CHUNK, D_MODEL), lambda i, c: (c, 0)),
            pl.BlockSpec((1, D_MODEL), const),
            pl.BlockSpec((1, D_MODEL), const),
        ],
        out_specs=pl.BlockSpec((bm, D_MODEL), row),
        out_shape=jax.ShapeDtypeStruct((m, D_MODEL), F32),
        scratch_shapes=[pltpu.VMEM((bm, D_MODEL), BF16)],
        compiler_params=_params(("parallel", "arbitrary")),
        name="merge_ln",
    )(h, o_sb, o_mla, w_gs, w_gm, b_gs, b_gm, w_bs, w_bm, w_o, g, b)


def _prep_weights(ffn1_w_in, ffn1_w_out, w_in, b_gate, w_uq, w_ukv, w_br_sb, w_br_mla, w_o, ffn2_w_in, ffn2_w_out):
    def ffn(w_i, w_out):
        return w_i.astype(BF16), w_out.astype(BF16)

    c0 = 3 * SB_WIDTH
    c1 = c0 + Q_LORA + KV_LORA + ROPE_DIM
    w_qkv = w_in[:, :c0].astype(BF16)
    w_lat = jnp.pad(w_in[:, c0:c1].astype(BF16), ((0, 0), (0, LANE - ROPE_DIM)))
    w_gs = w_in[:, c1:c1 + D_MODEL].astype(BF16)
    w_gm = w_in[:, c1 + D_MODEL:].astype(BF16)
    b_gs = b_gate[:D_MODEL].reshape(1, D_MODEL)
    b_gm = b_gate[D_MODEL:].reshape(1, D_MODEL)
    w_uq3 = w_uq.astype(BF16).reshape(Q_LORA, MLA_HEADS, NOPE_DIM + ROPE_DIM)
    w_uq_p = jnp.pad(w_uq3, ((0, 0), (0, 0), (0, 2 * LANE - NOPE_DIM - ROPE_DIM))).reshape(Q_LORA, 2 * LANE * MLA_HEADS)
    w_ukv3 = w_ukv.astype(BF16).reshape(KV_LORA, MLA_HEADS, NOPE_DIM + V_DIM)
    w_uk = w_ukv3[:, :, :NOPE_DIM].reshape(KV_LORA, MLA_HEADS * NOPE_DIM)
    w_uv = w_ukv3[:, :, NOPE_DIM:].reshape(KV_LORA, MLA_WIDTH)
    return dict(ffn1=ffn(ffn1_w_in, ffn1_w_out), ffn2=ffn(ffn2_w_in, ffn2_w_out),
                w_qkv=w_qkv, w_lat=w_lat, w_gs=w_gs, w_gm=w_gm, b_gs=b_gs, b_gm=b_gm,
                w_uq=w_uq_p, w_uk=w_uk, w_uv=w_uv,
                w_bs=w_br_sb.astype(BF16), w_bm=w_br_mla.astype(BF16), w_o=w_o.astype(BF16))


def _rope_freq():
    inv_freq = ROPE_THETA ** (-jnp.arange(0, ROPE_DIM, 2, dtype=F32) / ROPE_DIM)
    return jnp.concatenate([inv_freq, inv_freq, jnp.zeros((LANE - ROPE_DIM,), F32)]).reshape(1, LANE)


def _row2(v):
    return v.reshape(1, -1)


def kernel(x_prompt, x_sample, cache_sb_k, cache_sb_v, cache_mla_ckv, cache_mla_krope, ffn1_w_in, ffn1_w_out, ln1_g, ln1_b, w_in, b_gate, g_cq, w_uq, g_ckv, w_ukv, w_br_sb, w_br_mla, w_o, ln2_g, ln2_b, ffn2_w_in, ffn2_w_out, ln3_g, ln3_b):
    assert ffn1_w_in.shape[0] == DEPTH == 1
    b_p, t_p, _ = x_prompt.shape
    b_s, t_s, _ = x_sample.shape
    past = cache_sb_k.shape[2]
    w = _prep_weights(ffn1_w_in[0], ffn1_w_out[0], w_in[0], b_gate[0], w_uq[0], w_ukv[0],
                      w_br_sb[0], w_br_mla[0], w_o[0], ffn2_w_in[0], ffn2_w_out[0])
    freq = _rope_freq()
    ln1 = (_row2(ln1_g[0]), _row2(ln1_b[0]))
    ln2 = (_row2(ln2_g[0]), _row2(ln2_b[0]))
    ln3 = (_row2(ln3_g[0]), _row2(ln3_b[0]))
    g_cq2, g_ckv2 = _row2(g_cq[0]), _row2(g_ckv[0])

    def rowwise_front(x2, period, offset):
        m = x2.shape[0]
        h1 = _ffn_ln(x2, *w["ffn1"], *ln1, min(m, ROW_TILE_FFN))
        q, k, v, kb, vb = _qkv_proj(h1, w["w_qkv"], min(m, ROW_TILE_SMALL))
        qc, ckv, kr, krb, kn, vm = _lat_proj(h1, w["w_lat"], g_cq2, g_ckv2, w["w_uq"], w["w_uk"], w["w_uv"],
                                             freq, min(m, ROW_TILE), period, offset)
        return h1, (q, kb, vb), (qc, kn, krb, vm), (k, v, ckv, kr)

    def rowwise_back(h1, o_sb, o_mla):
        m = h1.shape[0]
        h2 = _merge_ln(h1, o_sb, o_mla, w["w_gs"], w["w_gm"], w["b_gs"], w["b_gm"], w["w_bs"], w["w_bm"],
                       w["w_o"], *ln2, min(m, ROW_TILE))
        return _ffn_ln(h2, *w["ffn2"], *ln3, min(m, ROW_TILE_FFN))

    m_p = b_p * t_p
    h1p, (q, kb, vb), (qc, kn, krb, vm), rows_p = rowwise_front(x_prompt.reshape(m_p, D_MODEL), t_p, 0)
    o_sb = _sb_prompt(q, kb, vb, b_p, t_p)
    o_mla = _mla_prompt(qc, kn, krb, vm, b_p, t_p)
    y_p = rowwise_back(h1p, o_sb, o_mla)

    m_s = b_s * t_s
    h1s, (q, kb, vb), (qc, kn, krb, vm), rows_s = rowwise_front(x_sample.reshape(m_s, D_MODEL), t_s, past)
    o_sb = _sb_sample(q, kb, vb, cache_sb_k.reshape(b_s, past * SB_HEADS, SB_HEAD_DIM),
                      cache_sb_v.reshape(b_s, past * SB_HEADS, SB_HEAD_DIM), b_s, t_s, past)
    o_mla = _mla_sample(qc, kn, krb, vm, cache_mla_ckv[0], cache_mla_krope[0], w["w_uk"], w["w_uv"], b_s, t_s, past)
    y_s = rowwise_back(h1s, o_sb, o_mla)

    def cache_rows(rows, b, t):
        k, v, ckv, kr = rows
        return (k.reshape(1, b, t, SB_HEADS, SB_HEAD_DIM), v.reshape(1, b, t, SB_HEADS, SB_HEAD_DIM),
                ckv.reshape(1, b, t, KV_LORA), kr.reshape(1, b, t, ROPE_DIM))

    return (y_p.reshape(b_p, t_p, D_MODEL), y_s.reshape(b_s, t_s, D_MODEL),
            *cache_rows(rows_p, b_p, t_p), *cache_rows(rows_s, b_s, t_s))
```

```python
import functools

import jax
import jax.numpy as jnp
from jax import lax
from jax.experimental import pallas as pl
from jax.experimental.pallas import tpu as pltpu

D_MODEL = 2048
DEPTH = 1
CHUNK = 64
SB_HEADS = 8
SB_HEAD_DIM = 128
SB_WIDTH = SB_HEADS * SB_HEAD_DIM
MLA_HEADS = 8
Q_LORA = 512
KV_LORA = 512
NOPE_DIM = 128
ROPE_DIM = 64
V_DIM = 128
MLA_WIDTH = MLA_HEADS * V_DIM
ROPE_THETA = 10000.0
D_FF = 5504
LN_EPS = 1e-5
RMS_EPS = 1e-6
ALPHA = (2 * DEPTH) ** 0.25
SB_SCALE = SB_HEAD_DIM ** -0.5
MLA_SCALE = (NOPE_DIM + ROPE_DIM) ** -0.5
MLA_SCALE_LOG2E = MLA_SCALE * 1.4426950408889634

LANE = 128
FF_CHUNK = 512
FF_STEPS = -(-D_FF // FF_CHUNK)
FF_OVERLAP = FF_STEPS * FF_CHUNK - D_FF
MERGE_CHUNK = 512
Q_TILE = 256
MLA_TILE = 512
ATTN_HEADS_PER_STEP = 2
SB_EXP_UNDERFLOW = 120.0
ROW_TILE_FFN = 1024
ROW_TILE = 512
ROW_TILE_SMALL = 256
VMEM_LIMIT = 56 * 1024 * 1024

BF16 = jnp.bfloat16
F32 = jnp.float32


def _params(sem):
    return pltpu.CompilerParams(dimension_semantics=sem, vmem_limit_bytes=VMEM_LIMIT)


def _dot(a, b):
    return jnp.dot(a, b, preferred_element_type=F32)


def _dot_nt(a, b):
    return lax.dot_general(a, b, (((1,), (1,)), ((), ())), preferred_element_type=F32)


def _layer_norm(r, g, b):
    mu = jnp.mean(r, axis=-1, keepdims=True)
    d = r - mu
    var = jnp.mean(d * d, axis=-1, keepdims=True)
    return d * lax.rsqrt(var + LN_EPS) * g + b


def _rms_norm(x, g):
    return x * lax.rsqrt(jnp.mean(x * x, axis=-1, keepdims=True) + RMS_EPS) * g


def _ffn_ln_kernel(x_ref, wg_ref, wu_ref, wo_ref, g_ref, b_ref, o_ref, xb_ref):
    j = pl.program_id(1)
    last = pl.num_programs(1) - 1

    @pl.when(j == 0)
    def _():
        xb_ref[...] = x_ref[...].astype(BF16)
        o_ref[...] = jnp.zeros_like(o_ref)

    xb = xb_ref[...]
    gate = _dot(xb, wg_ref[...])
    up = _dot(xb, wu_ref[...])
    act = gate * jax.nn.sigmoid(gate) * up
    col = lax.broadcasted_iota(jnp.int32, act.shape, 1)
    act = jnp.where((j < last) | (col >= FF_OVERLAP), act, 0.0).astype(BF16)
    o_ref[...] += _dot(act, wo_ref[...])

    @pl.when(j == last)
    def _():
        r = ALPHA * x_ref[...] + 0.5 * o_ref[...]
        o_ref[...] = _layer_norm(r, g_ref[...], b_ref[...])


def _ffn_ln(x, w_in, w_out, g, b, bm):
    m = x.shape[0]
    per = FF_CHUNK // LANE
    chunk_start = lambda j, base=0: (base // LANE + j * per - (j // (FF_STEPS - 1)) * (FF_OVERLAP // LANE)) * LANE
    return pl.pallas_call(
        _ffn_ln_kernel,
        grid=(m // bm, FF_STEPS),
        in_specs=[
            pl.BlockSpec((bm, D_MODEL), lambda i, j: (i, 0), pipeline_mode=pl.Buffered(1)),
            pl.BlockSpec((pl.Element(D_MODEL), pl.Element(FF_CHUNK)), lambda i, j: (0, chunk_start(j))),
            pl.BlockSpec((pl.Element(D_MODEL), pl.Element(FF_CHUNK)), lambda i, j: (0, chunk_start(j, D_FF))),
            pl.BlockSpec((pl.Element(FF_CHUNK), pl.Element(D_MODEL)), lambda i, j: (chunk_start(j), 0)),
            pl.BlockSpec((1, D_MODEL), lambda i, j: (0, 0)),
            pl.BlockSpec((1, D_MODEL), lambda i, j: (0, 0)),
        ],
        out_specs=pl.BlockSpec((bm, D_MODEL), lambda i, j: (i, 0)),
        out_shape=jax.ShapeDtypeStruct((m, D_MODEL), F32),
        scratch_shapes=[pltpu.VMEM((bm, D_MODEL), BF16)],
        compiler_params=_params(("parallel", "arbitrary")),
        name="ffn_ln",
    )(x, w_in, w_in, w_out, g, b)


def _qkv_kernel(h_ref, w_ref, q_ref, k_ref, v_ref, kb_ref, vb_ref):
    hb = h_ref[...].astype(BF16)
    q_ref[...] = _dot(hb, w_ref[:, 0:SB_WIDTH]).astype(BF16)
    k = _dot(hb, w_ref[:, SB_WIDTH:2 * SB_WIDTH])
    k_ref[...] = k
    kb_ref[...] = k.astype(BF16)
    v = _dot(hb, w_ref[:, 2 * SB_WIDTH:3 * SB_WIDTH])
    v_ref[...] = v
    vb_ref[...] = v.astype(BF16)


def _qkv_proj(h, w_qkv, bm):
    m = h.shape[0]
    row = lambda i: (i, 0)
    const = lambda i: (0, 0)
    return pl.pallas_call(
        _qkv_kernel,
        grid=(m // bm,),
        in_specs=[pl.BlockSpec((bm, D_MODEL), row), pl.BlockSpec((D_MODEL, 3 * SB_WIDTH), const)],
        out_specs=[pl.BlockSpec((bm, SB_WIDTH), row)] * 5,
        out_shape=[
            jax.ShapeDtypeStruct((m, SB_WIDTH), BF16),
            jax.ShapeDtypeStruct((m, SB_WIDTH), F32),
            jax.ShapeDtypeStruct((m, SB_WIDTH), F32),
            jax.ShapeDtypeStruct((m, SB_WIDTH), BF16),
            jax.ShapeDtypeStruct((m, SB_WIDTH), BF16),
        ],
        compiler_params=_params(("parallel",)),
        name="qkv_proj",
    )(h, w_qkv)


def _rope_lanes(x, cos_t, nsin_lo, sin_hi):
    return x * cos_t + pltpu.roll(x, LANE - ROPE_DIM // 2, 1) * nsin_lo + pltpu.roll(x, ROPE_DIM // 2, 1) * sin_hi


def _lat_kernel(h_ref, wl_ref, gq_ref, gkv_ref, wuq_ref, wuk_ref, wuv_ref, freq_ref,
                qc_ref, ckv_ref, kr_ref, krb_ref, kn_ref, vm_ref, *, bm, period, offset):
    hb = h_ref[...].astype(BF16)
    row = pl.program_id(0) * bm + lax.broadcasted_iota(jnp.int32, (bm, LANE), 0)
    pos = (jnp.bitwise_and(row, period - 1) + offset).astype(F32)
    ang = pos * freq_ref[...]
    lane = lax.broadcasted_iota(jnp.int32, (bm, LANE), 1)
    half = ROPE_DIM // 2
    sin_a = jnp.sin(ang)
    cos_t = jnp.where(lane < ROPE_DIM, jnp.cos(ang), 0.0)
    nsin_lo = jnp.where(lane < half, -sin_a, 0.0)
    sin_hi = jnp.where((lane >= half) & (lane < ROPE_DIM), sin_a, 0.0)

    c_q = _rms_norm(_dot(hb, wl_ref[:, 0:Q_LORA]), gq_ref[...]).astype(BF16)
    q_all = _dot(c_q, wuq_ref[...])
    for hd in range(MLA_HEADS):
        base = 2 * LANE * hd
        qc_ref[:, base:base + LANE] = q_all[:, base:base + LANE].astype(BF16)
        qr = _rope_lanes(q_all[:, base + LANE:base + 2 * LANE], cos_t, nsin_lo, sin_hi)
        qc_ref[:, base + LANE:base + 2 * LANE] = qr.astype(BF16)

    c_kv = _rms_norm(_dot(hb, wl_ref[:, Q_LORA:Q_LORA + KV_LORA]), gkv_ref[...])
    ckv_ref[...] = c_kv
    c_kvb = c_kv.astype(BF16)
    kn_ref[...] = _dot(c_kvb, wuk_ref[...]).astype(BF16)
    vm_ref[...] = _dot(c_kvb, wuv_ref[...]).astype(BF16)

    k_r = _rope_lanes(_dot(hb, wl_ref[:, Q_LORA + KV_LORA:Q_LORA + KV_LORA + LANE]), cos_t, nsin_lo, sin_hi)
    kr_ref[...] = k_r[:, 0:ROPE_DIM]
    krb_ref[...] = k_r.astype(BF16)


def _lat_proj(h, w_lat, g_cq, g_ckv, w_uq, w_uk, w_uv, freq, bm, period, offset):
    m = h.shape[0]
    assert period & (period - 1) == 0
    row = lambda i: (i, 0)
    const = lambda i: (0, 0)
    lat_cols = Q_LORA + KV_LORA + LANE
    return pl.pallas_call(
        functools.partial(_lat_kernel, bm=bm, period=period, offset=offset),
        grid=(m // bm,),
        in_specs=[
            pl.BlockSpec((bm, D_MODEL), row),
            pl.BlockSpec((D_MODEL, lat_cols), const),
            pl.BlockSpec((1, Q_LORA), const),
            pl.BlockSpec((1, KV_LORA), const),
            pl.BlockSpec((Q_LORA, 2 * LANE * MLA_HEADS), const),
            pl.BlockSpec((KV_LORA, MLA_HEADS * NOPE_DIM), const),
            pl.BlockSpec((KV_LORA, MLA_WIDTH), const),
            pl.BlockSpec((1, LANE), const),
        ],
        out_specs=[
            pl.BlockSpec((bm, 2 * LANE * MLA_HEADS), row),
            pl.BlockSpec((bm, KV_LORA), row),
            pl.BlockSpec((bm, ROPE_DIM), row),
            pl.BlockSpec((bm, LANE), row),
            pl.BlockSpec((bm, MLA_HEADS * NOPE_DIM), row),
            pl.BlockSpec((bm, MLA_WIDTH), row),
        ],
        out_shape=[
            jax.ShapeDtypeStruct((m, 2 * LANE * MLA_HEADS), BF16),
            jax.ShapeDtypeStruct((m, KV_LORA), F32),
            jax.ShapeDtypeStruct((m, ROPE_DIM), F32),
            jax.ShapeDtypeStruct((m, LANE), BF16),
            jax.ShapeDtypeStruct((m, MLA_HEADS * NOPE_DIM), BF16),
            jax.ShapeDtypeStruct((m, MLA_WIDTH), BF16),
        ],
        compiler_params=_params(("parallel",)),
        name="lat_proj",
    )(h, w_lat, g_cq, g_ckv, w_uq, w_uk, w_uv, freq)


def _strict_upper(n):
    j = lax.broadcasted_iota(jnp.int32, (n, n), 0)
    s = lax.broadcasted_iota(jnp.int32, (n, n), 1)
    return jnp.where(j > s, 1.0, 0.0).astype(BF16)


def _sb_step(q, k, v, u_tri, run, acc, vis):
    tile = u_tri.shape[0]
    z = _dot_nt(q, k) * SB_SCALE
    l1m = -(jnp.maximum(z, 0.0) + jnp.log(1.0 + jnp.exp(-jnp.abs(z))))
    if vis is not None:
        l1m = jnp.where(vis, l1m, 0.0)
    hi = l1m.astype(BF16)
    lo = (l1m - hi.astype(F32)).astype(BF16)
    tails = []
    for blk in reversed(range(k.shape[0] // tile)):
        cols = slice(blk * tile, (blk + 1) * tile)
        tails.insert(0, _dot(hi[:, cols], u_tri) + _dot(lo[:, cols], u_tri) + run)
        run = run + jnp.sum(l1m[:, cols], axis=-1, keepdims=True)
    tail = tails[0] if len(tails) == 1 else jnp.concatenate(tails, axis=1)
    a = jnp.exp(z + l1m + tail)
    if vis is not None:
        a = jnp.where(vis, a, 0.0)
    acc = acc + _dot(a.astype(BF16), v)
    return run, acc


def _sb_prompt_kernel(q_ref, k_ref, v_ref, o_ref, *, tile, heads):
    i = pl.program_id(2)
    u_tri = _strict_upper(tile)
    cols = [slice(g * SB_HEAD_DIM, (g + 1) * SB_HEAD_DIM) for g in range(heads)]
    qs = [q_ref[:, cg] for cg in cols]

    def sweep(s0, width, state, vis):
        return tuple(_sb_step(qs[g], k_ref[pl.ds(s0, width), cols[g]], v_ref[pl.ds(s0, width), cols[g]],
                              u_tri, *state[g], vis) for g in range(heads))

    def live(state):
        return functools.reduce(jnp.maximum, [jnp.max(st[0]) for st in state]) > -SB_EXP_UNDERFLOW

    first = jnp.maximum(i - 1, 0)
    k_pos = first * tile + lax.broadcasted_iota(jnp.int32, (tile, 2 * tile), 1)
    q_pos = i * tile + lax.broadcasted_iota(jnp.int32, (tile, 2 * tile), 0)
    init = tuple((jnp.zeros((tile, 1), F32), jnp.zeros((tile, SB_HEAD_DIM), F32)) for _ in range(heads))
    state = sweep(pl.multiple_of(first * tile, tile), 2 * tile, init, k_pos < q_pos)

    def cond(carry):
        return (carry[0] >= 0) & carry[2]

    def body(carry):
        t, state, _ = carry
        state = sweep(pl.multiple_of(t * tile, tile), tile, state, None)
        return t - 1, state, live(state)

    _, state, _ = lax.while_loop(cond, body, (first - 1, state, live(state)))
    for g in range(heads):
        o_ref[:, cols[g]] = state[g][1].astype(BF16)


def _sb_prompt(q, k, v, batch, seq):
    nq = seq // Q_TILE
    width = ATTN_HEADS_PER_STEP * SB_HEAD_DIM
    qmap = lambda b, h, i: (b * nq + i, h)
    kvmap = lambda b, h, i: (b, h)
    return pl.pallas_call(
        functools.partial(_sb_prompt_kernel, tile=Q_TILE, heads=ATTN_HEADS_PER_STEP),
        grid=(batch, SB_HEADS // ATTN_HEADS_PER_STEP, nq),
        in_specs=[
            pl.BlockSpec((Q_TILE, width), qmap),
            pl.BlockSpec((seq, width), kvmap),
            pl.BlockSpec((seq, width), kvmap),
        ],
        out_specs=pl.BlockSpec((Q_TILE, width), qmap),
        out_shape=jax.ShapeDtypeStruct((batch * seq, SB_WIDTH), BF16),
        compiler_params=_params(("parallel", "parallel", "arbitrary")),
        name="sb_prompt",
    )(q, k, v)


def _sb_sample_kernel(q_ref, kn_ref, vn_ref, kc_ref, vc_ref, o_ref, sfx_ref, *, t_new, past):
    n_keys = past + LANE
    n_lane = SB_HEADS * t_new

    @pl.when(pl.program_id(0) == 0)
    def _():
        s = lax.broadcasted_iota(jnp.int32, (n_keys, n_keys), 0)
        j = lax.broadcasted_iota(jnp.int32, (n_keys, n_keys), 1)
        sfx_ref[...] = jnp.where(j > s, 1.0, 0.0).astype(BF16)

    def all_heads(cache_ref, new_ref):
        cached = jnp.concatenate(
            [cache_ref[0, pl.ds(hd, past, stride=SB_HEADS), :] for hd in range(SB_HEADS)], axis=1)
        pad = jnp.zeros((LANE - t_new, SB_WIDTH), BF16)
        return jnp.concatenate([cached.astype(BF16), new_ref[...], pad], axis=0)

    q_rep = jnp.concatenate([q_ref[...]] * SB_HEADS, axis=0)
    row_head = lax.broadcasted_iota(jnp.int32, q_rep.shape, 0) // t_new
    col_head = lax.broadcasted_iota(jnp.int32, q_rep.shape, 1) // SB_HEAD_DIM
    q_bd = jnp.where(row_head == col_head, q_rep, jnp.zeros_like(q_rep))

    z = _dot_nt(all_heads(kc_ref, kn_ref), q_bd) * SB_SCALE
    key = lax.broadcasted_iota(jnp.int32, z.shape, 0)
    query = lax.broadcasted_iota(jnp.int32, z.shape, 1) % t_new
    vis = key < past + query
    l1m = jnp.where(vis, -(jnp.maximum(z, 0.0) + jnp.log(1.0 + jnp.exp(-jnp.abs(z)))), 0.0)
    hi = l1m.astype(BF16)
    lo = (l1m - hi.astype(F32)).astype(BF16)
    tail = _dot(sfx_ref[...], hi) + _dot(sfx_ref[...], lo)
    a = jnp.where(vis, jnp.exp(z + l1m + tail), 0.0)
    out = _dot(a.T.astype(BF16), all_heads(vc_ref, vn_ref))
    for hd in range(SB_HEADS):
        cols = slice(hd * SB_HEAD_DIM, (hd + 1) * SB_HEAD_DIM)
        o_ref[:, cols] = out[hd * t_new:(hd + 1) * t_new, cols].astype(BF16)


def _sb_sample(q, k_new, v_new, cache_k, cache_v, streams, t_new, past):
    assert SB_HEADS * t_new == LANE and t_new % 16 == 0
    n_keys = past + LANE
    row = lambda b: (b, 0)
    cmap = lambda b: (b, 0, 0)
    return pl.pallas_call(
        functools.partial(_sb_sample_kernel, t_new=t_new, past=past),
        grid=(streams,),
        in_specs=[
            pl.BlockSpec((t_new, SB_WIDTH), row),
            pl.BlockSpec((t_new, SB_WIDTH), row),
            pl.BlockSpec((t_new, SB_WIDTH), row),
            pl.BlockSpec((1, past * SB_HEADS, SB_HEAD_DIM), cmap),
            pl.BlockSpec((1, past * SB_HEADS, SB_HEAD_DIM), cmap),
        ],
        out_specs=pl.BlockSpec((t_new, SB_WIDTH), row),
        out_shape=jax.ShapeDtypeStruct((streams * t_new, SB_WIDTH), BF16),
        scratch_shapes=[pltpu.VMEM((n_keys, n_keys), BF16)],
        compiler_params=_params(("arbitrary",)),
        name="sb_sample",
    )(q, k_new, v_new, cache_k, cache_v)


def _softmax_step(q, kc, v, m, l, acc, vis):
    s = _dot_nt(q, kc)
    if vis is not None:
        s = jnp.where(vis, s, -jnp.inf)
    m_new = jnp.maximum(m, jnp.max(s, axis=-1, keepdims=True))
    corr = jnp.exp2((m - m_new) * MLA_SCALE_LOG2E)
    p = jnp.exp2((s - m_new) * MLA_SCALE_LOG2E)
    l = l * corr + jnp.sum(p, axis=-1, keepdims=True)
    acc = acc * corr + _dot(p.astype(BF16), v)
    return m_new, l, acc


def _mla_prompt_kernel(q_ref, kn_ref, kr_ref, v_ref, o_ref, *, tile, heads):
    i = pl.program_id(2)
    qs = [q_ref[:, 2 * LANE * g:2 * LANE * (g + 1)] for g in range(heads)]
    cols = [slice(g * NOPE_DIM, (g + 1) * NOPE_DIM) for g in range(heads)]

    def sweep(s0, state, vis):
        k_rope = kr_ref[pl.ds(s0, tile), :]
        return tuple(_softmax_step(qs[g], jnp.concatenate([kn_ref[pl.ds(s0, tile), cols[g]], k_rope], axis=1),
                                   v_ref[pl.ds(s0, tile), cols[g]], *state[g], vis) for g in range(heads))

    init = tuple((jnp.full((tile, 1), -jnp.inf, F32), jnp.zeros((tile, 1), F32), jnp.zeros((tile, V_DIM), F32))
                 for _ in range(heads))
    state = lax.fori_loop(0, i, lambda t, st: sweep(pl.multiple_of(t * tile, tile), st, None), init)
    r = lax.broadcasted_iota(jnp.int32, (tile, tile), 0)
    c = lax.broadcasted_iota(jnp.int32, (tile, tile), 1)
    state = sweep(pl.multiple_of(i * tile, tile), state, (c // CHUNK) <= (r // CHUNK))
    for g in range(heads):
        _, l, acc = state[g]
        o_ref[:, cols[g]] = (acc / l).astype(BF16)


def _mla_prompt(q_cat, k_nope, kr_pad, v, batch, seq):
    nq = seq // MLA_TILE
    hps = ATTN_HEADS_PER_STEP
    qmap = lambda b, h, i: (b * nq + i, h)
    kvmap = lambda b, h, i: (b, h)
    return pl.pallas_call(
        functools.partial(_mla_prompt_kernel, tile=MLA_TILE, heads=hps),
        grid=(batch, MLA_HEADS // hps, nq),
        in_specs=[
            pl.BlockSpec((MLA_TILE, 2 * LANE * hps), qmap),
            pl.BlockSpec((seq, NOPE_DIM * hps), kvmap),
            pl.BlockSpec((seq, LANE), lambda b, h, i: (b, 0)),
            pl.BlockSpec((seq, V_DIM * hps), kvmap),
        ],
        out_specs=pl.BlockSpec((MLA_TILE, V_DIM * hps), qmap),
        out_shape=jax.ShapeDtypeStruct((batch * seq, MLA_WIDTH), BF16),
        compiler_params=_params(("parallel", "parallel", "arbitrary")),
        name="mla_prompt",
    )(q_cat, k_nope, kr_pad, v)


def _mla_sample_kernel(q_ref, knn_ref, krn_ref, vn_ref, ckv_ref, krc_ref, wuk_ref, wuv_ref, o_ref, *, t_new):
    ckv = ckv_ref[0].astype(BF16)
    k_past = _dot(ckv, wuk_ref[...]).astype(BF16)
    v_past = _dot(ckv, wuv_ref[...]).astype(BF16)
    kr_past = krc_ref[0].astype(BF16)
    kr_new = krn_ref[...]
    for hd in range(MLA_HEADS):
        cols = slice(hd * NOPE_DIM, (hd + 1) * NOPE_DIM)
        q = q_ref[:, 2 * LANE * hd:2 * LANE * (hd + 1)]
        q_nope = q[:, 0:NOPE_DIM]
        q_rope = q[:, NOPE_DIM:NOPE_DIM + ROPE_DIM]
        s_p = (_dot_nt(q_nope, k_past[:, cols]) + _dot_nt(q_rope, kr_past)) * MLA_SCALE
        s_n = _dot_nt(q, jnp.concatenate([knn_ref[:, cols], kr_new], axis=1)) * MLA_SCALE
        m = jnp.maximum(jnp.max(s_p, axis=-1, keepdims=True), jnp.max(s_n, axis=-1, keepdims=True))
        p_p = jnp.exp(s_p - m)
        p_n = jnp.exp(s_n - m)
        l = jnp.sum(p_p, axis=-1, keepdims=True) + jnp.sum(p_n, axis=-1, keepdims=True)
        acc = _dot(p_p.astype(BF16), v_past[:, cols]) + _dot(p_n.astype(BF16), vn_ref[:, cols])
        o_ref[:, cols] = (acc / l).astype(BF16)


def _mla_sample(q_cat, kn_new, kr_new, v_new, cache_ckv, cache_kr, w_uk, w_uv, streams, t_new, past):
    row = lambda b: (b, 0)
    cmap = lambda b: (b, 0, 0)
    const = lambda b: (0, 0)
    return pl.pallas_call(
        functools.partial(_mla_sample_kernel, t_new=t_new),
        grid=(streams,),
        in_specs=[
            pl.BlockSpec((t_new, 2 * LANE * MLA_HEADS), row),
            pl.BlockSpec((t_new, MLA_HEADS * NOPE_DIM), row),
            pl.BlockSpec((t_new, LANE), row),
            pl.BlockSpec((t_new, MLA_WIDTH), row),
            pl.BlockSpec((1, past, KV_LORA), cmap),
            pl.BlockSpec((1, past, ROPE_DIM), cmap),
            pl.BlockSpec((KV_LORA, MLA_HEADS * NOPE_DIM), const),
            pl.BlockSpec((KV_LORA, MLA_WIDTH), const),
        ],
        out_specs=pl.BlockSpec((t_new, MLA_WIDTH), row),
        out_shape=jax.ShapeDtypeStruct((streams * t_new, MLA_WIDTH), BF16),
        compiler_params=_params(("parallel",)),
        name="mla_sample",
    )(q_cat, kn_new, kr_new, v_new, cache_ckv, cache_kr, w_uk, w_uv)


def _merge_ln_kernel(h_ref, osb_ref, omla_ref, wgs_ref, wgm_ref, bgs_ref, bgm_ref, wbs_ref, wbm_ref,
                     wo_ref, g_ref, b_ref, o_ref, hb_ref):
    c = pl.program_id(1)
    last = pl.num_programs(1) - 1

    @pl.when(c == 0)
    def _():
        hb_ref[...] = h_ref[...].astype(BF16)
        o_ref[...] = jnp.zeros_like(o_ref)

    hb = hb_ref[...]
    g_sb = jax.nn.sigmoid(_dot(hb, wgs_ref[...]) + bgs_ref[...])
    g_mla = jax.nn.sigmoid(_dot(hb, wgm_ref[...]) + bgm_ref[...])
    merged = g_sb * _dot(osb_ref[...], wbs_ref[...]) + g_mla * _dot(omla_ref[...], wbm_ref[...])
    o_ref[...] += _dot(merged.astype(BF16), wo_ref[...])

    @pl.when(c == last)
    def _():
        o_ref[...] = _layer_norm(ALPHA * h_ref[...] + o_ref[...], g_ref[...], b_ref[...])


def _merge_ln(h, o_sb, o_mla, w_gs, w_gm, b_gs, b_gm, w_bs, w_bm, w_o, g, b, bm):
    m = h.shape[0]
    nc = D_MODEL // MERGE_CHUNK
    row = lambda i, c: (i, 0)
    col = lambda i, c: (0, c)
    const = lambda i, c: (0, 0)
    return pl.pallas_call(
        _merge_ln_kernel,
        grid=(m // bm, nc),
        in_specs=[
            pl.BlockSpec((bm, D_MODEL), row),
            pl.BlockSpec((bm, SB_WIDTH), row),
            pl.BlockSpec((bm, MLA_WIDTH), row),
            pl.BlockSpec((D_MODEL, MERGE_CHUNK), col),
            pl.BlockSpec((D_MODEL, MERGE_CHUNK), col),
            pl.BlockSpec((1, MERGE_CHUNK), col),
            pl.BlockSpec((1, MERGE_CHUNK), col),
            pl.BlockSpec((SB_WIDTH, MERGE_CHUNK), col),
            pl.BlockSpec((MLA_WIDTH, MERGE_CHUNK), col),
            pl.BlockSpec((MERGE_CHUNK, D_MODEL), lambda i, c: (c, 0)),
            pl.BlockSpec((1, D_MODEL), const),
            pl.BlockSpec((1, D_MODEL), const),
        ],
        out_specs=pl.BlockSpec((bm, D_MODEL), row),
        out_shape=jax.ShapeDtypeStruct((m, D_MODEL), F32),
        scratch_shapes=[pltpu.VMEM((bm, D_MODEL), BF16)],
        compiler_params=_params(("parallel", "arbitrary")),
        name="merge_ln",
    )(h, o_sb, o_mla, w_gs, w_gm, b_gs, b_gm, w_bs, w_bm, w_o, g, b)


def _prep_weights(ffn1_w_in, ffn1_w_out, w_in, b_gate, w_uq, w_ukv, w_br_sb, w_br_mla, w_o, ffn2_w_in, ffn2_w_out):
    def ffn(w_i, w_out):
        return w_i.astype(BF16), w_out.astype(BF16)

    c0 = 3 * SB_WIDTH
    c1 = c0 + Q_LORA + KV_LORA + ROPE_DIM
    w_qkv = w_in[:, :c0].astype(BF16)
    w_lat = jnp.pad(w_in[:, c0:c1].astype(BF16), ((0, 0), (0, LANE - ROPE_DIM)))
    w_gs = w_in[:, c1:c1 + D_MODEL].astype(BF16)
    w_gm = w_in[:, c1 + D_MODEL:].astype(BF16)
    b_gs = b_gate[:D_MODEL].reshape(1, D_MODEL)
    b_gm = b_gate[D_MODEL:].reshape(1, D_MODEL)
    w_uq3 = w_uq.astype(BF16).reshape(Q_LORA, MLA_HEADS, NOPE_DIM + ROPE_DIM)
    w_uq_p = jnp.pad(w_uq3, ((0, 0), (0, 0), (0, 2 * LANE - NOPE_DIM - ROPE_DIM))).reshape(Q_LORA, 2 * LANE * MLA_HEADS)
    w_ukv3 = w_ukv.astype(BF16).reshape(KV_LORA, MLA_HEADS, NOPE_DIM + V_DIM)
    w_uk = w_ukv3[:, :, :NOPE_DIM].reshape(KV_LORA, MLA_HEADS * NOPE_DIM)
    w_uv = w_ukv3[:, :, NOPE_DIM:].reshape(KV_LORA, MLA_WIDTH)
    return dict(ffn1=ffn(ffn1_w_in, ffn1_w_out), ffn2=ffn(ffn2_w_in, ffn2_w_out),
                w_qkv=w_qkv, w_lat=w_lat, w_gs=w_gs, w_gm=w_gm, b_gs=b_gs, b_gm=b_gm,
                w_uq=w_uq_p, w_uk=w_uk, w_uv=w_uv,
                w_bs=w_br_sb.astype(BF16), w_bm=w_br_mla.astype(BF16), w_o=w_o.astype(BF16))


def _rope_freq():
    inv_freq = ROPE_THETA ** (-jnp.arange(0, ROPE_DIM, 2, dtype=F32) / ROPE_DIM)
    return jnp.concatenate([inv_freq, inv_freq, jnp.zeros((LANE - ROPE_DIM,), F32)]).reshape(1, LANE)


def _row2(v):
    return v.reshape(1, -1)


def kernel(x_prompt, x_sample, cache_sb_k, cache_sb_v, cache_mla_ckv, cache_mla_krope, ffn1_w_in, ffn1_w_out, ln1_g, ln1_b, w_in, b_gate, g_cq, w_uq, g_ckv, w_ukv, w_br_sb, w_br_mla, w_o, ln2_g, ln2_b, ffn2_w_in, ffn2_w_out, ln3_g, ln3_b):
    assert ffn1_w_in.shape[0] == DEPTH == 1
    b_p, t_p, _ = x_prompt.shape
    b_s, t_s, _ = x_sample.shape
    past = cache_sb_k.shape[2]
    w = _prep_weights(ffn1_w_in[0], ffn1_w_out[0], w_in[0], b_gate[0], w_uq[0], w_ukv[0],
                      w_br_sb[0], w_br_mla[0], w_o[0], ffn2_w_in[0], ffn2_w_out[0])
    freq = _rope_freq()
    ln1 = (_row2(ln1_g[0]), _row2(ln1_b[0]))
    ln2 = (_row2(ln2_g[0]), _row2(ln2_b[0]))
    ln3 = (_row2(ln3_g[0]), _row2(ln3_b[0]))
    g_cq2, g_ckv2 = _row2(g_cq[0]), _row2(g_ckv[0])

    def rowwise_front(x2, period, offset):
        m = x2.shape[0]
        h1 = _ffn_ln(x2, *w["ffn1"], *ln1, min(m, ROW_TILE_FFN))
        q, k, v, kb, vb = _qkv_proj(h1, w["w_qkv"], min(m, ROW_TILE_SMALL))
        qc, ckv, kr, krb, kn, vm = _lat_proj(h1, w["w_lat"], g_cq2, g_ckv2, w["w_uq"], w["w_uk"], w["w_uv"],
                                             freq, min(m, ROW_TILE), period, offset)
        return h1, (q, kb, vb), (qc, kn, krb, vm), (k, v, ckv, kr)

    def rowwise_back(h1, o_sb, o_mla):
        m = h1.shape[0]
        h2 = _merge_ln(h1, o_sb, o_mla, w["w_gs"], w["w_gm"], w["b_gs"], w["b_gm"], w["w_bs"], w["w_bm"],
                       w["w_o"], *ln2, min(m, ROW_TILE))
        return _ffn_ln(h2, *w["ffn2"], *ln3, min(m, ROW_TILE_FFN))

    m_p = b_p * t_p
    h1p, (q, kb, vb), (qc, kn, krb, vm), rows_p = rowwise_front(x_prompt.reshape(m_p, D_MODEL), t_p, 0)
    o_sb = _sb_prompt(q, kb, vb, b_p, t_p)
    o_mla = _mla_prompt(qc, kn, krb, vm, b_p, t_p)
    y_p = rowwise_back(h1p, o_sb, o_mla)

    m_s = b_s * t_s
    h1s, (q, kb, vb), (qc, kn, krb, vm), rows_s = rowwise_front(x_sample.reshape(m_s, D_MODEL), t_s, past)
    o_sb = _sb_sample(q, kb, vb, cache_sb_k.reshape(b_s, past * SB_HEADS, SB_HEAD_DIM),
                      cache_sb_v.reshape(b_s, past * SB_HEADS, SB_HEAD_DIM), b_s, t_s, past)
    o_mla = _mla_sample(qc, kn, krb, vm, cache_mla_ckv[0], cache_mla_krope[0], w["w_uk"], w["w_uv"], b_s, t_s, past)
    y_s = rowwise_back(h1s, o_sb, o_mla)

    def cache_rows(rows, b, t):
        k, v, ckv, kr = rows
        return (k.reshape(1, b, t, SB_HEADS, SB_HEAD_DIM), v.reshape(1, b, t, SB_HEADS, SB_HEAD_DIM),
                ckv.reshape(1, b, t, KV_LORA), kr.reshape(1, b, t, ROPE_DIM))

    return (y_p.reshape(b_p, t_p, D_MODEL), y_s.reshape(b_s, t_s, D_MODEL),
            *cache_rows(rows_p, b_p, t_p), *cache_rows(rows_s, b_s, t_s))
```

```python
import functools

import jax
import jax.numpy as jnp
from jax import lax
from jax.experimental import pallas as pl
from jax.experimental.pallas import tpu as pltpu

D_MODEL = 2048
DEPTH = 1
CHUNK = 64
SB_HEADS = 8
SB_HEAD_DIM = 128
SB_WIDTH = SB_HEADS * SB_HEAD_DIM
MLA_HEADS = 8
Q_LORA = 512
KV_LORA = 512
NOPE_DIM = 128
ROPE_DIM = 64
V_DIM = 128
MLA_WIDTH = MLA_HEADS * V_DIM
ROPE_THETA = 10000.0
D_FF = 5504
LN_EPS = 1e-5
RMS_EPS = 1e-6
ALPHA = (2 * DEPTH) ** 0.25
SB_SCALE = SB_HEAD_DIM ** -0.5
MLA_SCALE = (NOPE_DIM + ROPE_DIM) ** -0.5
MLA_SCALE_LOG2E = MLA_SCALE * 1.4426950408889634

LANE = 128
FF_CHUNK = 512
FF_CHUNK_FEW_ROWS = 1408
MERGE_CHUNK = 512
Q_TILE = 256
MLA_TILE = 512
ATTN_HEADS_PER_STEP = 4
SB_EXP_UNDERFLOW = 120.0
ROW_TILE = 512
ROW_TILE_SMALL = 256
VMEM_LIMIT = 56 * 1024 * 1024

BF16 = jnp.bfloat16
F32 = jnp.float32


def _params(sem):
    return pltpu.CompilerParams(dimension_semantics=sem, vmem_limit_bytes=VMEM_LIMIT)


def _dot(a, b):
    return jnp.dot(a, b, preferred_element_type=F32)


def _dot_nt(a, b):
    return lax.dot_general(a, b, (((1,), (1,)), ((), ())), preferred_element_type=F32)


def _layer_norm(r, g, b):
    mu = jnp.mean(r, axis=-1, keepdims=True)
    d = r - mu
    var = jnp.mean(d * d, axis=-1, keepdims=True)
    return d * lax.rsqrt(var + LN_EPS) * g + b


def _rms_norm(x, g):
    return x * lax.rsqrt(jnp.mean(x * x, axis=-1, keepdims=True) + RMS_EPS) * g


def _ffn_ln_kernel(x_ref, wg_ref, wu_ref, wo_ref, g_ref, b_ref, o_ref, xb_ref, *, overlap):
    j = pl.program_id(1)
    last = pl.num_programs(1) - 1

    @pl.when(j == 0)
    def _():
        xb_ref[...] = x_ref[...].astype(BF16)
        o_ref[...] = jnp.zeros_like(o_ref)

    xb = xb_ref[...]
    gate = _dot(xb, wg_ref[...])
    up = _dot(xb, wu_ref[...])
    act = gate * jax.nn.sigmoid(gate) * up
    col = lax.broadcasted_iota(jnp.int32, act.shape, 1)
    act = jnp.where((j < last) | (col >= overlap), act, 0.0).astype(BF16)
    o_ref[...] += _dot(act, wo_ref[...])

    @pl.when(j == last)
    def _():
        r = ALPHA * x_ref[...] + 0.5 * o_ref[...]
        o_ref[...] = _layer_norm(r, g_ref[...], b_ref[...])


def _ffn_ln(x, w_in, w_out, g, b, bm, chunk):
    m = x.shape[0]
    steps = -(-D_FF // chunk)
    overlap = steps * chunk - D_FF
    assert steps > 1 and chunk % LANE == 0 and overlap % LANE == 0 and overlap < chunk
    chunk_start = lambda j, base=0: (base // LANE + j * (chunk // LANE)
                                     - (j // (steps - 1)) * (overlap // LANE)) * LANE
    return pl.pallas_call(
        functools.partial(_ffn_ln_kernel, overlap=overlap),
        grid=(m // bm, steps),
        in_specs=[
            pl.BlockSpec((bm, D_MODEL), lambda i, j: (i, 0)),
            pl.BlockSpec((pl.Element(D_MODEL), pl.Element(chunk)), lambda i, j: (0, chunk_start(j))),
            pl.BlockSpec((pl.Element(D_MODEL), pl.Element(chunk)), lambda i, j: (0, chunk_start(j, D_FF))),
            pl.BlockSpec((pl.Element(chunk), pl.Element(D_MODEL)), lambda i, j: (chunk_start(j), 0)),
            pl.BlockSpec((1, D_MODEL), lambda i, j: (0, 0)),
            pl.BlockSpec((1, D_MODEL), lambda i, j: (0, 0)),
        ],
        out_specs=pl.BlockSpec((bm, D_MODEL), lambda i, j: (i, 0)),
        out_shape=jax.ShapeDtypeStruct((m, D_MODEL), F32),
        scratch_shapes=[pltpu.VMEM((bm, D_MODEL), BF16)],
        compiler_params=_params(("parallel", "arbitrary")),
        name="ffn_ln",
    )(x, w_in, w_in, w_out, g, b)


def _qkv_kernel(h_ref, w_ref, q_ref, k_ref, v_ref, kb_ref, vb_ref):
    hb = h_ref[...].astype(BF16)
    q_ref[...] = _dot(hb, w_ref[:, 0:SB_WIDTH]).astype(BF16)
    k = _dot(hb, w_ref[:, SB_WIDTH:2 * SB_WIDTH])
    k_ref[...] = k
    kb_ref[...] = k.astype(BF16)
    v = _dot(hb, w_ref[:, 2 * SB_WIDTH:3 * SB_WIDTH])
    v_ref[...] = v
    vb_ref[...] = v.astype(BF16)


def _qkv_proj(h, w_qkv, bm):
    m = h.shape[0]
    row = lambda i: (i, 0)
    const = lambda i: (0, 0)
    return pl.pallas_call(
        _qkv_kernel,
        grid=(m // bm,),
        in_specs=[pl.BlockSpec((bm, D_MODEL), row), pl.BlockSpec((D_MODEL, 3 * SB_WIDTH), const)],
        out_specs=[pl.BlockSpec((bm, SB_WIDTH), row)] * 5,
        out_shape=[
            jax.ShapeDtypeStruct((m, SB_WIDTH), BF16),
            jax.ShapeDtypeStruct((m, SB_WIDTH), F32),
            jax.ShapeDtypeStruct((m, SB_WIDTH), F32),
            jax.ShapeDtypeStruct((m, SB_WIDTH), BF16),
            jax.ShapeDtypeStruct((m, SB_WIDTH), BF16),
        ],
        compiler_params=_params(("parallel",)),
        name="qkv_proj",
    )(h, w_qkv)


def _rope_lanes(x, cos_t, nsin_lo, sin_hi):
    return x * cos_t + pltpu.roll(x, LANE - ROPE_DIM // 2, 1) * nsin_lo + pltpu.roll(x, ROPE_DIM // 2, 1) * sin_hi


def _lat_kernel(h_ref, wl_ref, gq_ref, gkv_ref, wuq_ref, wuk_ref, wuv_ref, freq_ref,
                qc_ref, ckv_ref, kr_ref, krb_ref, kn_ref, vm_ref, *, bm, period, offset):
    hb = h_ref[...].astype(BF16)
    row = pl.program_id(0) * bm + lax.broadcasted_iota(jnp.int32, (bm, LANE), 0)
    pos = (jnp.bitwise_and(row, period - 1) + offset).astype(F32)
    ang = pos * freq_ref[...]
    lane = lax.broadcasted_iota(jnp.int32, (bm, LANE), 1)
    half = ROPE_DIM // 2
    sin_a = jnp.sin(ang)
    cos_t = jnp.where(lane < ROPE_DIM, jnp.cos(ang), 0.0)
    nsin_lo = jnp.where(lane < half, -sin_a, 0.0)
    sin_hi = jnp.where((lane >= half) & (lane < ROPE_DIM), sin_a, 0.0)

    c_q = _rms_norm(_dot(hb, wl_ref[:, 0:Q_LORA]), gq_ref[...]).astype(BF16)
    q_all = _dot(c_q, wuq_ref[...])
    for hd in range(MLA_HEADS):
        base = 2 * LANE * hd
        qc_ref[:, base:base + LANE] = q_all[:, base:base + LANE].astype(BF16)
        qr = _rope_lanes(q_all[:, base + LANE:base + 2 * LANE], cos_t, nsin_lo, sin_hi)
        qc_ref[:, base + LANE:base + 2 * LANE] = qr.astype(BF16)

    c_kv = _rms_norm(_dot(hb, wl_ref[:, Q_LORA:Q_LORA + KV_LORA]), gkv_ref[...])
    ckv_ref[...] = c_kv
    c_kvb = c_kv.astype(BF16)
    kn_ref[...] = _dot(c_kvb, wuk_ref[...]).astype(BF16)
    vm_ref[...] = _dot(c_kvb, wuv_ref[...]).astype(BF16)

    k_r = jnp.where(lane < ROPE_DIM, _dot(hb, wl_ref[:, Q_LORA + KV_LORA:Q_LORA + KV_LORA + LANE]), 0.0)
    k_r = _rope_lanes(k_r, cos_t, nsin_lo, sin_hi)
    kr_ref[...] = k_r[:, 0:ROPE_DIM]
    krb_ref[...] = k_r.astype(BF16)


def _lat_proj(h, w_lat, g_cq, g_ckv, w_uq, w_uk, w_uv, freq, bm, period, offset):
    m = h.shape[0]
    assert period & (period - 1) == 0
    row = lambda i: (i, 0)
    const = lambda i: (0, 0)
    lat_cols = Q_LORA + KV_LORA + LANE
    return pl.pallas_call(
        functools.partial(_lat_kernel, bm=bm, period=period, offset=offset),
        grid=(m // bm,),
        in_specs=[
            pl.BlockSpec((bm, D_MODEL), row),
            pl.BlockSpec((pl.Element(D_MODEL), pl.Element(lat_cols)), lambda i: (0, 3 * SB_WIDTH)),
            pl.BlockSpec((1, Q_LORA), const),
            pl.BlockSpec((1, KV_LORA), const),
            pl.BlockSpec((Q_LORA, 2 * LANE * MLA_HEADS), const),
            pl.BlockSpec((KV_LORA, MLA_HEADS * NOPE_DIM), const),
            pl.BlockSpec((KV_LORA, MLA_WIDTH), const),
            pl.BlockSpec((1, LANE), const),
        ],
        out_specs=[
            pl.BlockSpec((bm, 2 * LANE * MLA_HEADS), row),
            pl.BlockSpec((bm, KV_LORA), row),
            pl.BlockSpec((bm, ROPE_DIM), row),
            pl.BlockSpec((bm, LANE), row),
            pl.BlockSpec((bm, MLA_HEADS * NOPE_DIM), row),
            pl.BlockSpec((bm, MLA_WIDTH), row),
        ],
        out_shape=[
            jax.ShapeDtypeStruct((m, 2 * LANE * MLA_HEADS), BF16),
            jax.ShapeDtypeStruct((m, KV_LORA), F32),
            jax.ShapeDtypeStruct((m, ROPE_DIM), F32),
            jax.ShapeDtypeStruct((m, LANE), BF16),
            jax.ShapeDtypeStruct((m, MLA_HEADS * NOPE_DIM), BF16),
            jax.ShapeDtypeStruct((m, MLA_WIDTH), BF16),
        ],
        compiler_params=_params(("parallel",)),
        name="lat_proj",
    )(h, w_lat, g_cq, g_ckv, w_uq, w_uk, w_uv, freq)


def _strict_upper(n):
    j = lax.broadcasted_iota(jnp.int32, (n, n), 0)
    s = lax.broadcasted_iota(jnp.int32, (n, n), 1)
    return jnp.where(j > s, 1.0, 0.0).astype(BF16)


def _sb_step(q, k, v, u_tri, run, acc, vis):
    tile = u_tri.shape[0]
    z = _dot_nt(q, k) * SB_SCALE
    l1m = -(jnp.maximum(z, 0.0) + jnp.log(1.0 + jnp.exp(-jnp.abs(z))))
    if vis is not None:
        l1m = jnp.where(vis, l1m, 0.0)
    hi = l1m.astype(BF16)
    lo = (l1m - hi.astype(F32)).astype(BF16)
    tails = []
    for blk in reversed(range(k.shape[0] // tile)):
        cols = slice(blk * tile, (blk + 1) * tile)
        tails.insert(0, _dot(hi[:, cols], u_tri) + _dot(lo[:, cols], u_tri) + run)
        run = run + jnp.sum(l1m[:, cols], axis=-1, keepdims=True)
    tail = tails[0] if len(tails) == 1 else jnp.concatenate(tails, axis=1)
    a = jnp.exp(z + l1m + tail)
    if vis is not None:
        a = jnp.where(vis, a, 0.0)
    acc = acc + _dot(a.astype(BF16), v)
    return run, acc


def _sb_prompt_kernel(q_ref, k_ref, v_ref, o_ref, *, tile, heads):
    i = pl.program_id(2)
    u_tri = _strict_upper(tile)
    cols = [slice(g * SB_HEAD_DIM, (g + 1) * SB_HEAD_DIM) for g in range(heads)]
    qs = [q_ref[:, cg] for cg in cols]

    def sweep(s0, width, state, vis):
        return tuple(_sb_step(qs[g], k_ref[pl.ds(s0, width), cols[g]], v_ref[pl.ds(s0, width), cols[g]],
                              u_tri, *state[g], vis) for g in range(heads))

    def live(state):
        return functools.reduce(jnp.maximum, [jnp.max(st[0]) for st in state]) > -SB_EXP_UNDERFLOW

    first = jnp.maximum(i - 1, 0)
    k_pos = first * tile + lax.broadcasted_iota(jnp.int32, (tile, 2 * tile), 1)
    q_pos = i * tile + lax.broadcasted_iota(jnp.int32, (tile, 2 * tile), 0)
    init = tuple((jnp.zeros((tile, 1), F32), jnp.zeros((tile, SB_HEAD_DIM), F32)) for _ in range(heads))
    state = sweep(pl.multiple_of(first * tile, tile), 2 * tile, init, k_pos < q_pos)

    def cond(carry):
        return (carry[0] >= 0) & carry[2]

    def body(carry):
        t, state, _ = carry
        state = sweep(pl.multiple_of(t * tile, tile), tile, state, None)
        return t - 1, state, live(state)

    _, state, _ = lax.while_loop(cond, body, (first - 1, state, live(state)))
    for g in range(heads):
        o_ref[:, cols[g]] = state[g][1].astype(BF16)


def _sb_prompt(q, k, v, batch, seq):
    nq = seq // Q_TILE
    width = ATTN_HEADS_PER_STEP * SB_HEAD_DIM
    qmap = lambda b, h, i: (b * nq + i, h)
    kvmap = lambda b, h, i: (b, h)
    return pl.pallas_call(
        functools.partial(_sb_prompt_kernel, tile=Q_TILE, heads=ATTN_HEADS_PER_STEP),
        grid=(batch, SB_HEADS // ATTN_HEADS_PER_STEP, nq),
        in_specs=[
            pl.BlockSpec((Q_TILE, width), qmap),
            pl.BlockSpec((seq, width), kvmap),
            pl.BlockSpec((seq, width), kvmap),
        ],
        out_specs=pl.BlockSpec((Q_TILE, width), qmap),
        out_shape=jax.ShapeDtypeStruct((batch * seq, SB_WIDTH), BF16),
        compiler_params=_params(("parallel", "parallel", "arbitrary")),
        name="sb_prompt",
    )(q, k, v)


def _sb_sample_kernel(q_ref, kn_ref, vn_ref, kc_ref, vc_ref, o_ref, *, t_new, past):
    n_keys = past + LANE
    s_idx = lax.broadcasted_iota(jnp.int32, (LANE, LANE), 0)
    j_idx = lax.broadcasted_iota(jnp.int32, (LANE, LANE), 1)
    sfx = jnp.where(j_idx > s_idx, 1.0, 0.0).astype(BF16)

    def all_heads(cache_ref, new_ref):
        cached = jnp.concatenate(
            [cache_ref[0, pl.ds(hd, past, stride=SB_HEADS), :] for hd in range(SB_HEADS)], axis=1)
        pad = jnp.zeros((LANE - t_new, SB_WIDTH), BF16)
        return jnp.concatenate([cached.astype(BF16), new_ref[...], pad], axis=0)

    q_rep = jnp.concatenate([q_ref[...]] * SB_HEADS, axis=0)
    row_head = lax.broadcasted_iota(jnp.int32, q_rep.shape, 0) // t_new
    col_head = lax.broadcasted_iota(jnp.int32, q_rep.shape, 1) // SB_HEAD_DIM
    q_bd = jnp.where(row_head == col_head, q_rep, jnp.zeros_like(q_rep))

    z = _dot_nt(all_heads(kc_ref, kn_ref), q_bd) * SB_SCALE
    key = lax.broadcasted_iota(jnp.int32, z.shape, 0)
    query = lax.broadcasted_iota(jnp.int32, z.shape, 1) % t_new
    vis = key < past + query
    l1m = jnp.where(vis, -(jnp.maximum(z, 0.0) + jnp.log(1.0 + jnp.exp(-jnp.abs(z)))), 0.0)
    hi = l1m.astype(BF16)
    lo = (l1m - hi.astype(F32)).astype(BF16)
    tails = []
    right = jnp.zeros((1, z.shape[1]), F32)
    for blk in reversed(range(n_keys // LANE)):
        rows = slice(blk * LANE, (blk + 1) * LANE)
        tails.insert(0, _dot(sfx, hi[rows]) + _dot(sfx, lo[rows]) + right)
        right = right + jnp.sum(l1m[rows], axis=0, keepdims=True)
    a = jnp.where(vis, jnp.exp(z + l1m + jnp.concatenate(tails, axis=0)), 0.0)
    out = _dot(a.T.astype(BF16), all_heads(vc_ref, vn_ref))
    for hd in range(SB_HEADS):
        cols = slice(hd * SB_HEAD_DIM, (hd + 1) * SB_HEAD_DIM)
        o_ref[:, cols] = out[hd * t_new:(hd + 1) * t_new, cols].astype(BF16)


def _sb_sample(q, k_new, v_new, cache_k, cache_v, streams, t_new, past):
    assert SB_HEADS * t_new == LANE and t_new % 16 == 0 and past % LANE == 0
    row = lambda b: (b, 0)
    cmap = lambda b: (b, 0, 0)
    return pl.pallas_call(
        functools.partial(_sb_sample_kernel, t_new=t_new, past=past),
        grid=(streams,),
        in_specs=[
            pl.BlockSpec((t_new, SB_WIDTH), row),
            pl.BlockSpec((t_new, SB_WIDTH), row),
            pl.BlockSpec((t_new, SB_WIDTH), row),
            pl.BlockSpec((1, past * SB_HEADS, SB_HEAD_DIM), cmap),
            pl.BlockSpec((1, past * SB_HEADS, SB_HEAD_DIM), cmap),
        ],
        out_specs=pl.BlockSpec((t_new, SB_WIDTH), row),
        out_shape=jax.ShapeDtypeStruct((streams * t_new, SB_WIDTH), BF16),
        compiler_params=_params(("parallel",)),
        name="sb_sample",
    )(q, k_new, v_new, cache_k, cache_v)


def _softmax_step(q, kc, v, m, l, acc, vis):
    s = _dot_nt(q, kc)
    if vis is not None:
        s = jnp.where(vis, s, -jnp.inf)
    m_new = jnp.maximum(m, jnp.max(s, axis=-1, keepdims=True))
    corr = jnp.exp2((m - m_new) * MLA_SCALE_LOG2E)
    p = jnp.exp2((s - m_new) * MLA_SCALE_LOG2E)
    l = l * corr + jnp.sum(p, axis=-1, keepdims=True)
    acc = acc * corr + _dot(p.astype(BF16), v)
    return m_new, l, acc


def _mla_prompt_kernel(q_ref, kn_ref, kr_ref, v_ref, o_ref, *, tile, heads):
    i = pl.program_id(2)
    qs = [q_ref[:, 2 * LANE * g:2 * LANE * (g + 1)] for g in range(heads)]
    cols = [slice(g * NOPE_DIM, (g + 1) * NOPE_DIM) for g in range(heads)]

    def sweep(s0, state, vis):
        k_rope = kr_ref[pl.ds(s0, tile), :]
        return tuple(_softmax_step(qs[g], jnp.concatenate([kn_ref[pl.ds(s0, tile), cols[g]], k_rope], axis=1),
                                   v_ref[pl.ds(s0, tile), cols[g]], *state[g], vis) for g in range(heads))

    init = tuple((jnp.full((tile, 1), -jnp.inf, F32), jnp.zeros((tile, 1), F32), jnp.zeros((tile, V_DIM), F32))
                 for _ in range(heads))
    state = lax.fori_loop(0, i, lambda t, st: sweep(pl.multiple_of(t * tile, tile), st, None), init)
    r = lax.broadcasted_iota(jnp.int32, (tile, tile), 0)
    c = lax.broadcasted_iota(jnp.int32, (tile, tile), 1)
    state = sweep(pl.multiple_of(i * tile, tile), state, (c // CHUNK) <= (r // CHUNK))
    for g in range(heads):
        _, l, acc = state[g]
        o_ref[:, cols[g]] = (acc / l).astype(BF16)


def _mla_prompt(q_cat, k_nope, kr_pad, v, batch, seq):
    nq = seq // MLA_TILE
    hps = ATTN_HEADS_PER_STEP
    qmap = lambda b, h, i: (b * nq + i, h)
    kvmap = lambda b, h, i: (b, h)
    return pl.pallas_call(
        functools.partial(_mla_prompt_kernel, tile=MLA_TILE, heads=hps),
        grid=(batch, MLA_HEADS // hps, nq),
        in_specs=[
            pl.BlockSpec((MLA_TILE, 2 * LANE * hps), qmap),
            pl.BlockSpec((seq, NOPE_DIM * hps), kvmap),
            pl.BlockSpec((seq, LANE), lambda b, h, i: (b, 0)),
            pl.BlockSpec((seq, V_DIM * hps), kvmap),
        ],
        out_specs=pl.BlockSpec((MLA_TILE, V_DIM * hps), qmap),
        out_shape=jax.ShapeDtypeStruct((batch * seq, MLA_WIDTH), BF16),
        compiler_params=_params(("parallel", "parallel", "arbitrary")),
        name="mla_prompt",
    )(q_cat, k_nope, kr_pad, v)


def _mla_sample_kernel(q_ref, knn_ref, krn_ref, vn_ref, ckv_ref, krc_ref, wuk_ref, wuv_ref, o_ref, *, t_new):
    ckv = ckv_ref[0].astype(BF16)
    k_past = _dot(ckv, wuk_ref[...]).astype(BF16)
    v_past = _dot(ckv, wuv_ref[...]).astype(BF16)
    kr_past = krc_ref[0].astype(BF16)
    kr_new = krn_ref[...]
    for hd in range(MLA_HEADS):
        cols = slice(hd * NOPE_DIM, (hd + 1) * NOPE_DIM)
        q = q_ref[:, 2 * LANE * hd:2 * LANE * (hd + 1)]
        q_nope = q[:, 0:NOPE_DIM]
        q_rope = q[:, NOPE_DIM:NOPE_DIM + ROPE_DIM]
        s_p = (_dot_nt(q_nope, k_past[:, cols]) + _dot_nt(q_rope, kr_past)) * MLA_SCALE
        s_n = _dot_nt(q, jnp.concatenate([knn_ref[:, cols], kr_new], axis=1)) * MLA_SCALE
        m = jnp.maximum(jnp.max(s_p, axis=-1, keepdims=True), jnp.max(s_n, axis=-1, keepdims=True))
        p_p = jnp.exp(s_p - m)
        p_n = jnp.exp(s_n - m)
        l = jnp.sum(p_p, axis=-1, keepdims=True) + jnp.sum(p_n, axis=-1, keepdims=True)
        acc = _dot(p_p.astype(BF16), v_past[:, cols]) + _dot(p_n.astype(BF16), vn_ref[:, cols])
        o_ref[:, cols] = (acc / l).astype(BF16)


def _mla_sample(q_cat, kn_new, kr_new, v_new, cache_ckv, cache_kr, w_uk, w_uv, streams, t_new, past):
    row = lambda b: (b, 0)
    cmap = lambda b: (b, 0, 0)
    const = lambda b: (0, 0)
    return pl.pallas_call(
        functools.partial(_mla_sample_kernel, t_new=t_new),
        grid=(streams,),
        in_specs=[
            pl.BlockSpec((t_new, 2 * LANE * MLA_HEADS), row),
            pl.BlockSpec((t_new, MLA_HEADS * NOPE_DIM), row),
            pl.BlockSpec((t_new, LANE), row),
            pl.BlockSpec((t_new, MLA_WIDTH), row),
            pl.BlockSpec((1, past, KV_LORA), cmap),
            pl.BlockSpec((1, past, ROPE_DIM), cmap),
            pl.BlockSpec((KV_LORA, MLA_HEADS * NOPE_DIM), const),
            pl.BlockSpec((KV_LORA, MLA_WIDTH), const),
        ],
        out_specs=pl.BlockSpec((t_new, MLA_WIDTH), row),
        out_shape=jax.ShapeDtypeStruct((streams * t_new, MLA_WIDTH), BF16),
        compiler_params=_params(("parallel",)),
        name="mla_sample",
    )(q_cat, kn_new, kr_new, v_new, cache_ckv, cache_kr, w_uk, w_uv)


def _merge_ln_kernel(h_ref, osb_ref, omla_ref, wgs_ref, wgm_ref, bgs_ref, bgm_ref, wbs_ref, wbm_ref,
                     wo_ref, g_ref, b_ref, o_ref, hb_ref):
    c = pl.program_id(1)
    last = pl.num_programs(1) - 1

    @pl.when(c == 0)
    def _():
        hb_ref[...] = h_ref[...].astype(BF16)
        o_ref[...] = jnp.zeros_like(o_ref)

    hb = hb_ref[...]
    g_sb = jax.nn.sigmoid(_dot(hb, wgs_ref[...]) + bgs_ref[...])
    g_mla = jax.nn.sigmoid(_dot(hb, wgm_ref[...]) + bgm_ref[...])
    merged = g_sb * _dot(osb_ref[...], wbs_ref[...]) + g_mla * _dot(omla_ref[...], wbm_ref[...])
    o_ref[...] += _dot(merged.astype(BF16), wo_ref[...])

    @pl.when(c == last)
    def _():
        o_ref[...] = _layer_norm(ALPHA * h_ref[...] + o_ref[...], g_ref[...], b_ref[...])


def _merge_ln(h, o_sb, o_mla, w_gs, w_gm, b_gs, b_gm, w_bs, w_bm, w_o, g, b, bm):
    m = h.shape[0]
    nc = D_MODEL // MERGE_CHUNK
    row = lambda i, c: (i, 0)
    col = lambda i, c: (0, c)
    const = lambda i, c: (0, 0)
    return pl.pallas_call(
        _merge_ln_kernel,
        grid=(m // bm, nc),
        in_specs=[
            pl.BlockSpec((bm, D_MODEL), row),
            pl.BlockSpec((bm, SB_WIDTH), row),
            pl.BlockSpec((bm, MLA_WIDTH), row),
            pl.BlockSpec((D_MODEL, MERGE_CHUNK), col),
            pl.BlockSpec((D_MODEL, MERGE_CHUNK), col),
            pl.BlockSpec((1, MERGE_CHUNK), col),
            pl.BlockSpec((1, MERGE_CHUNK), col),
            pl.BlockSpec((SB_WIDTH, MERGE_CHUNK), col),
            pl.BlockSpec((MLA_WIDTH, MERGE_CHUNK), col),
            pl.BlockSpec((MERGE_CHUNK, D_MODEL), lambda i, c: (c, 0)),
            pl.BlockSpec((1, D_MODEL), const),
            pl.BlockSpec((1, D_MODEL), const),
        ],
        out_specs=pl.BlockSpec((bm, D_MODEL), row),
        out_shape=jax.ShapeDtypeStruct((m, D_MODEL), F32),
        scratch_shapes=[pltpu.VMEM((bm, D_MODEL), BF16)],
        compiler_params=_params(("parallel", "arbitrary")),
        name="merge_ln",
    )(h, o_sb, o_mla, w_gs, w_gm, b_gs, b_gm, w_bs, w_bm, w_o, g, b)


def _prep_weights(ffn1_w_in, ffn1_w_out, w_in, b_gate, w_uq, w_ukv, w_br_sb, w_br_mla, w_o, ffn2_w_in, ffn2_w_out):
    def ffn(w_i, w_out):
        return w_i.astype(BF16), w_out.astype(BF16)

    c0 = 3 * SB_WIDTH
    c1 = c0 + Q_LORA + KV_LORA + ROPE_DIM
    w_in_b = w_in.astype(BF16)
    w_gs = w_in_b[:, c1:c1 + D_MODEL]
    w_gm = w_in_b[:, c1 + D_MODEL:]
    b_gs = b_gate[:D_MODEL].reshape(1, D_MODEL)
    b_gm = b_gate[D_MODEL:].reshape(1, D_MODEL)
    w_uq3 = w_uq.astype(BF16).reshape(Q_LORA, MLA_HEADS, NOPE_DIM + ROPE_DIM)
    w_uq_p = jnp.pad(w_uq3, ((0, 0), (0, 0), (0, 2 * LANE - NOPE_DIM - ROPE_DIM))).reshape(Q_LORA, 2 * LANE * MLA_HEADS)
    w_ukv3 = w_ukv.astype(BF16).reshape(KV_LORA, MLA_HEADS, NOPE_DIM + V_DIM)
    w_uk = w_ukv3[:, :, :NOPE_DIM].reshape(KV_LORA, MLA_HEADS * NOPE_DIM)
    w_uv = w_ukv3[:, :, NOPE_DIM:].reshape(KV_LORA, MLA_WIDTH)
    return dict(ffn1=ffn(ffn1_w_in, ffn1_w_out), ffn2=ffn(ffn2_w_in, ffn2_w_out),
                w_in=w_in_b, w_gs=w_gs, w_gm=w_gm, b_gs=b_gs, b_gm=b_gm,
                w_uq=w_uq_p, w_uk=w_uk, w_uv=w_uv,
                w_bs=w_br_sb.astype(BF16), w_bm=w_br_mla.astype(BF16), w_o=w_o.astype(BF16))


def _rope_freq():
    inv_freq = ROPE_THETA ** (-jnp.arange(0, ROPE_DIM, 2, dtype=F32) / ROPE_DIM)
    return jnp.concatenate([inv_freq, inv_freq, jnp.zeros((LANE - ROPE_DIM,), F32)]).reshape(1, LANE)


def _row2(v):
    return v.reshape(1, -1)


def kernel(x_prompt, x_sample, cache_sb_k, cache_sb_v, cache_mla_ckv, cache_mla_krope, ffn1_w_in, ffn1_w_out, ln1_g, ln1_b, w_in, b_gate, g_cq, w_uq, g_ckv, w_ukv, w_br_sb, w_br_mla, w_o, ln2_g, ln2_b, ffn2_w_in, ffn2_w_out, ln3_g, ln3_b):
    assert ffn1_w_in.shape[0] == DEPTH == 1
    b_p, t_p, _ = x_prompt.shape
    b_s, t_s, _ = x_sample.shape
    past = cache_sb_k.shape[2]
    w = _prep_weights(ffn1_w_in[0], ffn1_w_out[0], w_in[0], b_gate[0], w_uq[0], w_ukv[0],
                      w_br_sb[0], w_br_mla[0], w_o[0], ffn2_w_in[0], ffn2_w_out[0])
    freq = _rope_freq()
    ln1 = (_row2(ln1_g[0]), _row2(ln1_b[0]))
    ln2 = (_row2(ln2_g[0]), _row2(ln2_b[0]))
    ln3 = (_row2(ln3_g[0]), _row2(ln3_b[0]))
    g_cq2, g_ckv2 = _row2(g_cq[0]), _row2(g_ckv[0])

    def ff_chunk(m):
        return FF_CHUNK if m >= ROW_TILE else FF_CHUNK_FEW_ROWS

    def rowwise_front(x2, period, offset):
        m = x2.shape[0]
        h1 = _ffn_ln(x2, *w["ffn1"], *ln1, min(m, ROW_TILE), ff_chunk(m))
        q, k, v, kb, vb = _qkv_proj(h1, w["w_in"], min(m, ROW_TILE_SMALL))
        qc, ckv, kr, krb, kn, vm = _lat_proj(h1, w["w_in"], g_cq2, g_ckv2, w["w_uq"], w["w_uk"], w["w_uv"],
                                             freq, min(m, ROW_TILE), period, offset)
        return h1, (q, kb, vb), (qc, kn, krb, vm), (k, v, ckv, kr)

    def rowwise_back(h1, o_sb, o_mla):
        m = h1.shape[0]
        h2 = _merge_ln(h1, o_sb, o_mla, w["w_gs"], w["w_gm"], w["b_gs"], w["b_gm"], w["w_bs"], w["w_bm"],
                       w["w_o"], *ln2, min(m, ROW_TILE))
        return _ffn_ln(h2, *w["ffn2"], *ln3, min(m, ROW_TILE), ff_chunk(m))

    m_p = b_p * t_p
    h1p, (q, kb, vb), (qc, kn, krb, vm), rows_p = rowwise_front(x_prompt.reshape(m_p, D_MODEL), t_p, 0)
    o_sb = _sb_prompt(q, kb, vb, b_p, t_p)
    o_mla = _mla_prompt(qc, kn, krb, vm, b_p, t_p)
    y_p = rowwise_back(h1p, o_sb, o_mla)

    m_s = b_s * t_s
    h1s, (q, kb, vb), (qc, kn, krb, vm), rows_s = rowwise_front(x_sample.reshape(m_s, D_MODEL), t_s, past)
    o_sb = _sb_sample(q, kb, vb, cache_sb_k.reshape(b_s, past * SB_HEADS, SB_HEAD_DIM),
                      cache_sb_v.reshape(b_s, past * SB_HEADS, SB_HEAD_DIM), b_s, t_s, past)
    o_mla = _mla_sample(qc, kn, krb, vm, cache_mla_ckv[0], cache_mla_krope[0], w["w_uk"], w["w_uv"], b_s, t_s, past)
    y_s = rowwise_back(h1s, o_sb, o_mla)

    def cache_rows(rows, b, t):
        k, v, ckv, kr = rows
        return (k.reshape(1, b, t, SB_HEADS, SB_HEAD_DIM), v.reshape(1, b, t, SB_HEADS, SB_HEAD_DIM),
                ckv.reshape(1, b, t, KV_LORA), kr.reshape(1, b, t, ROPE_DIM))

    return (y_p.reshape(b_p, t_p, D_MODEL), y_s.reshape(b_s, t_s, D_MODEL),
            *cache_rows(rows_p, b_p, t_p), *cache_rows(rows_s, b_s, t_s))
```

```python
import functools

import jax
import jax.numpy as jnp
from jax import lax
from jax.experimental import pallas as pl
from jax.experimental.pallas import tpu as pltpu

D_MODEL = 2048
DEPTH = 1
CHUNK = 64
SB_HEADS = 8
SB_HEAD_DIM = 128
SB_WIDTH = SB_HEADS * SB_HEAD_DIM
MLA_HEADS = 8
Q_LORA = 512
KV_LORA = 512
NOPE_DIM = 128
ROPE_DIM = 64
V_DIM = 128
MLA_WIDTH = MLA_HEADS * V_DIM
ROPE_THETA = 10000.0
D_FF = 5504
LN_EPS = 1e-5
RMS_EPS = 1e-6
ALPHA = (2 * DEPTH) ** 0.25
SB_SCALE = SB_HEAD_DIM ** -0.5
MLA_SCALE = (NOPE_DIM + ROPE_DIM) ** -0.5
MLA_SCALE_LOG2E = MLA_SCALE * 1.4426950408889634

LANE = 128
FF_CHUNK = 512
FF_CHUNK_FEW_ROWS = 1408
MERGE_CHUNK = 512
Q_TILE = 256
MLA_TILE = 512
ATTN_HEADS_PER_STEP = 4
SB_EXP_UNDERFLOW = 120.0
ROW_TILE = 512
ROW_TILE_SMALL = 256
VMEM_LIMIT = 56 * 1024 * 1024

BF16 = jnp.bfloat16
F32 = jnp.float32


def _params(sem):
    return pltpu.CompilerParams(dimension_semantics=sem, vmem_limit_bytes=VMEM_LIMIT)


def _dot(a, b):
    return jnp.dot(a, b, preferred_element_type=F32)


def _dot_nt(a, b):
    return lax.dot_general(a, b, (((1,), (1,)), ((), ())), preferred_element_type=F32)


def _layer_norm(r, g, b):
    mu = jnp.mean(r, axis=-1, keepdims=True)
    d = r - mu
    var = jnp.mean(d * d, axis=-1, keepdims=True)
    return d * lax.rsqrt(var + LN_EPS) * g + b


def _rms_norm(x, g):
    return x * lax.rsqrt(jnp.mean(x * x, axis=-1, keepdims=True) + RMS_EPS) * g


def _ffn_ln_kernel(*refs, overlap, n_cast):
    x_ref, wg_ref, wu_ref, wo_ref, g_ref, b_ref = refs[:6]
    o_ref = refs[6 + n_cast]
    xb_ref = refs[-1]
    j = pl.program_id(1)
    last = pl.num_programs(1) - 1

    for src_ref, dst_ref in zip(refs[6:6 + n_cast], refs[7 + n_cast:7 + 2 * n_cast]):
        dst_ref[...] = src_ref[...].astype(BF16)

    @pl.when(j == 0)
    def _():
        xb_ref[...] = x_ref[...].astype(BF16)
        o_ref[...] = jnp.zeros_like(o_ref)

    xb = xb_ref[...]
    gate = _dot(xb, wg_ref[...])
    up = _dot(xb, wu_ref[...])
    act = gate * jax.nn.sigmoid(gate) * up
    col = lax.broadcasted_iota(jnp.int32, act.shape, 1)
    act = jnp.where((j < last) | (col >= overlap), act, 0.0).astype(BF16)
    o_ref[...] += _dot(act, wo_ref[...])

    @pl.when(j == last)
    def _():
        r = ALPHA * x_ref[...] + 0.5 * o_ref[...]
        o_ref[...] = _layer_norm(r, g_ref[...], b_ref[...])


def _ffn_ln(x, w_in, w_out, g, b, bm, chunk, cast=()):
    m = x.shape[0]
    steps = -(-D_FF // chunk)
    overlap = steps * chunk - D_FF
    assert steps > 1 and chunk % LANE == 0 and overlap % LANE == 0 and overlap < chunk
    chunk_start = lambda j, base=0: (base // LANE + j * (chunk // LANE)
                                     - (j // (steps - 1)) * (overlap // LANE)) * LANE
    cast_specs, cast_shapes = [], []
    for arr, rows in cast:
        n_blocks = arr.shape[0] // rows
        assert arr.shape[0] == n_blocks * rows and n_blocks <= (m // bm) * steps and rows % 16 == 0
        cast_specs.append(pl.BlockSpec(
            (rows, arr.shape[1]), lambda i, j, n=n_blocks: (jnp.minimum(i * steps + j, n - 1), 0)))
        cast_shapes.append(jax.ShapeDtypeStruct(arr.shape, BF16))
    outs = pl.pallas_call(
        functools.partial(_ffn_ln_kernel, overlap=overlap, n_cast=len(cast)),
        grid=(m // bm, steps),
        in_specs=[
            pl.BlockSpec((bm, D_MODEL), lambda i, j: (i, 0)),
            pl.BlockSpec((pl.Element(D_MODEL), pl.Element(chunk)), lambda i, j: (0, chunk_start(j))),
            pl.BlockSpec((pl.Element(D_MODEL), pl.Element(chunk)), lambda i, j: (0, chunk_start(j, D_FF))),
            pl.BlockSpec((pl.Element(chunk), pl.Element(D_MODEL)), lambda i, j: (chunk_start(j), 0)),
            pl.BlockSpec((1, D_MODEL), lambda i, j: (0, 0)),
            pl.BlockSpec((1, D_MODEL), lambda i, j: (0, 0)),
        ] + cast_specs,
        out_specs=[pl.BlockSpec((bm, D_MODEL), lambda i, j: (i, 0))] + cast_specs,
        out_shape=[jax.ShapeDtypeStruct((m, D_MODEL), F32)] + cast_shapes,
        scratch_shapes=[pltpu.VMEM((bm, D_MODEL), BF16)],
        compiler_params=_params(("arbitrary", "arbitrary") if cast else ("parallel", "arbitrary")),
        name="ffn_ln",
    )(x, w_in, w_in, w_out, g, b, *[arr for arr, _ in cast])
    return outs[0], list(outs[1:])


def _qkv_kernel(h_ref, w_ref, q_ref, k_ref, v_ref, kb_ref, vb_ref):
    hb = h_ref[...].astype(BF16)
    q_ref[...] = _dot(hb, w_ref[:, 0:SB_WIDTH]).astype(BF16)
    k = _dot(hb, w_ref[:, SB_WIDTH:2 * SB_WIDTH])
    k_ref[...] = k
    kb_ref[...] = k.astype(BF16)
    v = _dot(hb, w_ref[:, 2 * SB_WIDTH:3 * SB_WIDTH])
    v_ref[...] = v
    vb_ref[...] = v.astype(BF16)


def _qkv_proj(h, w_qkv, bm):
    m = h.shape[0]
    row = lambda i: (i, 0)
    const = lambda i: (0, 0)
    return pl.pallas_call(
        _qkv_kernel,
        grid=(m // bm,),
        in_specs=[pl.BlockSpec((bm, D_MODEL), row), pl.BlockSpec((D_MODEL, 3 * SB_WIDTH), const)],
        out_specs=[pl.BlockSpec((bm, SB_WIDTH), row)] * 5,
        out_shape=[
            jax.ShapeDtypeStruct((m, SB_WIDTH), BF16),
            jax.ShapeDtypeStruct((m, SB_WIDTH), F32),
            jax.ShapeDtypeStruct((m, SB_WIDTH), F32),
            jax.ShapeDtypeStruct((m, SB_WIDTH), BF16),
            jax.ShapeDtypeStruct((m, SB_WIDTH), BF16),
        ],
        compiler_params=_params(("parallel",)),
        name="qkv_proj",
    )(h, w_qkv)


def _rope_lanes(x, cos_t, nsin_lo, sin_hi):
    return x * cos_t + pltpu.roll(x, LANE - ROPE_DIM // 2, 1) * nsin_lo + pltpu.roll(x, ROPE_DIM // 2, 1) * sin_hi


def _lat_kernel(h_ref, wl_ref, gq_ref, gkv_ref, wuq_ref, wuk_ref, wuv_ref, freq_ref,
                qc_ref, ckv_ref, kr_ref, krb_ref, kn_ref, vm_ref, *, bm, period, offset):
    hb = h_ref[...].astype(BF16)
    row = pl.program_id(0) * bm + lax.broadcasted_iota(jnp.int32, (bm, LANE), 0)
    pos = (jnp.bitwise_and(row, period - 1) + offset).astype(F32)
    ang = pos * freq_ref[...]
    lane = lax.broadcasted_iota(jnp.int32, (bm, LANE), 1)
    half = ROPE_DIM // 2
    sin_a = jnp.sin(ang)
    cos_t = jnp.where(lane < ROPE_DIM, jnp.cos(ang), 0.0)
    nsin_lo = jnp.where(lane < half, -sin_a, 0.0)
    sin_hi = jnp.where((lane >= half) & (lane < ROPE_DIM), sin_a, 0.0)

    c_q = _rms_norm(_dot(hb, wl_ref[:, 0:Q_LORA]), gq_ref[...]).astype(BF16)
    q_all = _dot(c_q, wuq_ref[...])
    for hd in range(MLA_HEADS):
        base = 2 * LANE * hd
        qc_ref[:, base:base + LANE] = q_all[:, base:base + LANE].astype(BF16)
        qr = _rope_lanes(q_all[:, base + LANE:base + 2 * LANE], cos_t, nsin_lo, sin_hi)
        qc_ref[:, base + LANE:base + 2 * LANE] = qr.astype(BF16)

    c_kv = _rms_norm(_dot(hb, wl_ref[:, Q_LORA:Q_LORA + KV_LORA]), gkv_ref[...])
    ckv_ref[...] = c_kv
    c_kvb = c_kv.astype(BF16)
    kn_ref[...] = _dot(c_kvb, wuk_ref[...]).astype(BF16)
    vm_ref[...] = _dot(c_kvb, wuv_ref[...]).astype(BF16)

    k_r = jnp.where(lane < ROPE_DIM, _dot(hb, wl_ref[:, Q_LORA + KV_LORA:Q_LORA + KV_LORA + LANE]), 0.0)
    k_r = _rope_lanes(k_r, cos_t, nsin_lo, sin_hi)
    kr_ref[...] = k_r[:, 0:ROPE_DIM]
    krb_ref[...] = k_r.astype(BF16)


def _lat_proj(h, w_lat, g_cq, g_ckv, w_uq, w_uk, w_uv, freq, bm, period, offset):
    m = h.shape[0]
    assert period & (period - 1) == 0
    row = lambda i: (i, 0)
    const = lambda i: (0, 0)
    lat_cols = Q_LORA + KV_LORA + LANE
    return pl.pallas_call(
        functools.partial(_lat_kernel, bm=bm, period=period, offset=offset),
        grid=(m // bm,),
        in_specs=[
            pl.BlockSpec((bm, D_MODEL), row),
            pl.BlockSpec((pl.Element(D_MODEL), pl.Element(lat_cols)), lambda i: (0, 3 * SB_WIDTH)),
            pl.BlockSpec((1, Q_LORA), const),
            pl.BlockSpec((1, KV_LORA), const),
            pl.BlockSpec((Q_LORA, 2 * LANE * MLA_HEADS), const),
            pl.BlockSpec((KV_LORA, MLA_HEADS * NOPE_DIM), const),
            pl.BlockSpec((KV_LORA, MLA_WIDTH), const),
            pl.BlockSpec((1, LANE), const),
        ],
        out_specs=[
            pl.BlockSpec((bm, 2 * LANE * MLA_HEADS), row),
            pl.BlockSpec((bm, KV_LORA), row),
            pl.BlockSpec((bm, ROPE_DIM), row),
            pl.BlockSpec((bm, LANE), row),
            pl.BlockSpec((bm, MLA_HEADS * NOPE_DIM), row),
            pl.BlockSpec((bm, MLA_WIDTH), row),
        ],
        out_shape=[
            jax.ShapeDtypeStruct((m, 2 * LANE * MLA_HEADS), BF16),
            jax.ShapeDtypeStruct((m, KV_LORA), F32),
            jax.ShapeDtypeStruct((m, ROPE_DIM), F32),
            jax.ShapeDtypeStruct((m, LANE), BF16),
            jax.ShapeDtypeStruct((m, MLA_HEADS * NOPE_DIM), BF16),
            jax.ShapeDtypeStruct((m, MLA_WIDTH), BF16),
        ],
        compiler_params=_params(("parallel",)),
        name="lat_proj",
    )(h, w_lat, g_cq, g_ckv, w_uq, w_uk, w_uv, freq)


def _strict_upper(n):
    j = lax.broadcasted_iota(jnp.int32, (n, n), 0)
    s = lax.broadcasted_iota(jnp.int32, (n, n), 1)
    return jnp.where(j > s, 1.0, 0.0).astype(BF16)


def _sb_step(q, k, v, u_tri, run, acc, vis):
    tile = u_tri.shape[0]
    z = _dot_nt(q, k) * SB_SCALE
    l1m = -(jnp.maximum(z, 0.0) + jnp.log(1.0 + jnp.exp(-jnp.abs(z))))
    if vis is not None:
        l1m = jnp.where(vis, l1m, 0.0)
    hi = l1m.astype(BF16)
    lo = (l1m - hi.astype(F32)).astype(BF16)
    tails = []
    for blk in reversed(range(k.shape[0] // tile)):
        cols = slice(blk * tile, (blk + 1) * tile)
        tails.insert(0, _dot(hi[:, cols], u_tri) + _dot(lo[:, cols], u_tri) + run)
        run = run + jnp.sum(l1m[:, cols], axis=-1, keepdims=True)
    tail = tails[0] if len(tails) == 1 else jnp.concatenate(tails, axis=1)
    a = jnp.exp(z + l1m + tail)
    if vis is not None:
        a = jnp.where(vis, a, 0.0)
    acc = acc + _dot(a.astype(BF16), v)
    return run, acc


def _sb_prompt_kernel(q_ref, k_ref, v_ref, o_ref, *, tile, heads):
    i = pl.program_id(2)
    u_tri = _strict_upper(tile)
    cols = [slice(g * SB_HEAD_DIM, (g + 1) * SB_HEAD_DIM) for g in range(heads)]
    qs = [q_ref[:, cg] for cg in cols]

    def sweep(s0, width, state, vis):
        return tuple(_sb_step(qs[g], k_ref[pl.ds(s0, width), cols[g]], v_ref[pl.ds(s0, width), cols[g]],
                              u_tri, *state[g], vis) for g in range(heads))

    def live(state):
        return functools.reduce(jnp.maximum, [jnp.max(st[0]) for st in state]) > -SB_EXP_UNDERFLOW

    first = jnp.maximum(i - 1, 0)
    k_pos = first * tile + lax.broadcasted_iota(jnp.int32, (tile, 2 * tile), 1)
    q_pos = i * tile + lax.broadcasted_iota(jnp.int32, (tile, 2 * tile), 0)
    init = tuple((jnp.zeros((tile, 1), F32), jnp.zeros((tile, SB_HEAD_DIM), F32)) for _ in range(heads))
    state = sweep(pl.multiple_of(first * tile, tile), 2 * tile, init, k_pos < q_pos)

    def cond(carry):
        return (carry[0] >= 0) & carry[2]

    def body(carry):
        t, state, _ = carry
        state = sweep(pl.multiple_of(t * tile, tile), tile, state, None)
        return t - 1, state, live(state)

    _, state, _ = lax.while_loop(cond, body, (first - 1, state, live(state)))
    for g in range(heads):
        o_ref[:, cols[g]] = state[g][1].astype(BF16)


def _sb_prompt(q, k, v, batch, seq):
    nq = seq // Q_TILE
    width = ATTN_HEADS_PER_STEP * SB_HEAD_DIM
    qmap = lambda b, h, i: (b * nq + i, h)
    kvmap = lambda b, h, i: (b, h)
    return pl.pallas_call(
        functools.partial(_sb_prompt_kernel, tile=Q_TILE, heads=ATTN_HEADS_PER_STEP),
        grid=(batch, SB_HEADS // ATTN_HEADS_PER_STEP, nq),
        in_specs=[
            pl.BlockSpec((Q_TILE, width), qmap),
            pl.BlockSpec((seq, width), kvmap),
            pl.BlockSpec((seq, width), kvmap),
        ],
        out_specs=pl.BlockSpec((Q_TILE, width), qmap),
        out_shape=jax.ShapeDtypeStruct((batch * seq, SB_WIDTH), BF16),
        compiler_params=_params(("parallel", "parallel", "arbitrary")),
        name="sb_prompt",
    )(q, k, v)


def _sb_sample_kernel(q_ref, kn_ref, vn_ref, kc_ref, vc_ref, o_ref, *, t_new, past):
    n_keys = past + LANE
    s_idx = lax.broadcasted_iota(jnp.int32, (LANE, LANE), 0)
    j_idx = lax.broadcasted_iota(jnp.int32, (LANE, LANE), 1)
    sfx = jnp.where(j_idx > s_idx, 1.0, 0.0).astype(BF16)

    def all_heads(cache_ref, new_ref):
        cached = jnp.concatenate(
            [cache_ref[0, pl.ds(hd, past, stride=SB_HEADS), :] for hd in range(SB_HEADS)], axis=1)
        pad = jnp.zeros((LANE - t_new, SB_WIDTH), BF16)
        return jnp.concatenate([cached.astype(BF16), new_ref[...], pad], axis=0)

    q_rep = jnp.concatenate([q_ref[...]] * SB_HEADS, axis=0)
    row_head = lax.broadcasted_iota(jnp.int32, q_rep.shape, 0) // t_new
    col_head = lax.broadcasted_iota(jnp.int32, q_rep.shape, 1) // SB_HEAD_DIM
    q_bd = jnp.where(row_head == col_head, q_rep, jnp.zeros_like(q_rep))

    z = _dot_nt(all_heads(kc_ref, kn_ref), q_bd) * SB_SCALE
    key = lax.broadcasted_iota(jnp.int32, z.shape, 0)
    query = lax.broadcasted_iota(jnp.int32, z.shape, 1) % t_new
    vis = key < past + query
    l1m = jnp.where(vis, -(jnp.maximum(z, 0.0) + jnp.log(1.0 + jnp.exp(-jnp.abs(z)))), 0.0)
    hi = l1m.astype(BF16)
    lo = (l1m - hi.astype(F32)).astype(BF16)
    tails = []
    right = jnp.zeros((1, z.shape[1]), F32)
    for blk in reversed(range(n_keys // LANE)):
        rows = slice(blk * LANE, (blk + 1) * LANE)
        tails.insert(0, _dot(sfx, hi[rows]) + _dot(sfx, lo[rows]) + right)
        right = right + jnp.sum(l1m[rows], axis=0, keepdims=True)
    a = jnp.where(vis, jnp.exp(z + l1m + jnp.concatenate(tails, axis=0)), 0.0)
    out = _dot(a.T.astype(BF16), all_heads(vc_ref, vn_ref))
    for hd in range(SB_HEADS):
        cols = slice(hd * SB_HEAD_DIM, (hd + 1) * SB_HEAD_DIM)
        o_ref[:, cols] = out[hd * t_new:(hd + 1) * t_new, cols].astype(BF16)


def _sb_sample(q, k_new, v_new, cache_k, cache_v, streams, t_new, past):
    assert SB_HEADS * t_new == LANE and t_new % 16 == 0 and past % LANE == 0
    row = lambda b: (b, 0)
    cmap = lambda b: (b, 0, 0)
    return pl.pallas_call(
        functools.partial(_sb_sample_kernel, t_new=t_new, past=past),
        grid=(streams,),
        in_specs=[
            pl.BlockSpec((t_new, SB_WIDTH), row),
            pl.BlockSpec((t_new, SB_WIDTH), row),
            pl.BlockSpec((t_new, SB_WIDTH), row),
            pl.BlockSpec((1, past * SB_HEADS, SB_HEAD_DIM), cmap),
            pl.BlockSpec((1, past * SB_HEADS, SB_HEAD_DIM), cmap),
        ],
        out_specs=pl.BlockSpec((t_new, SB_WIDTH), row),
        out_shape=jax.ShapeDtypeStruct((streams * t_new, SB_WIDTH), BF16),
        compiler_params=_params(("parallel",)),
        name="sb_sample",
    )(q, k_new, v_new, cache_k, cache_v)


def _softmax_step(q, kc, v, m, l, acc, vis):
    s = _dot_nt(q, kc)
    if vis is not None:
        s = jnp.where(vis, s, -jnp.inf)
    m_new = jnp.maximum(m, jnp.max(s, axis=-1, keepdims=True))
    corr = jnp.exp2((m - m_new) * MLA_SCALE_LOG2E)
    p = jnp.exp2((s - m_new) * MLA_SCALE_LOG2E)
    l = l * corr + jnp.sum(p, axis=-1, keepdims=True)
    acc = acc * corr + _dot(p.astype(BF16), v)
    return m_new, l, acc


def _mla_prompt_kernel(q_ref, kn_ref, kr_ref, v_ref, o_ref, *, tile, heads):
    i = pl.program_id(2)
    qs = [q_ref[:, 2 * LANE * g:2 * LANE * (g + 1)] for g in range(heads)]
    cols = [slice(g * NOPE_DIM, (g + 1) * NOPE_DIM) for g in range(heads)]

    def sweep(s0, state, vis):
        k_rope = kr_ref[pl.ds(s0, tile), :]
        return tuple(_softmax_step(qs[g], jnp.concatenate([kn_ref[pl.ds(s0, tile), cols[g]], k_rope], axis=1),
                                   v_ref[pl.ds(s0, tile), cols[g]], *state[g], vis) for g in range(heads))

    init = tuple((jnp.full((tile, 1), -jnp.inf, F32), jnp.zeros((tile, 1), F32), jnp.zeros((tile, V_DIM), F32))
                 for _ in range(heads))
    state = lax.fori_loop(0, i, lambda t, st: sweep(pl.multiple_of(t * tile, tile), st, None), init)
    r = lax.broadcasted_iota(jnp.int32, (tile, tile), 0)
    c = lax.broadcasted_iota(jnp.int32, (tile, tile), 1)
    state = sweep(pl.multiple_of(i * tile, tile), state, (c // CHUNK) <= (r // CHUNK))
    for g in range(heads):
        _, l, acc = state[g]
        o_ref[:, cols[g]] = (acc / l).astype(BF16)


def _mla_prompt(q_cat, k_nope, kr_pad, v, batch, seq):
    nq = seq // MLA_TILE
    hps = ATTN_HEADS_PER_STEP
    qmap = lambda b, h, i: (b * nq + i, h)
    kvmap = lambda b, h, i: (b, h)
    return pl.pallas_call(
        functools.partial(_mla_prompt_kernel, tile=MLA_TILE, heads=hps),
        grid=(batch, MLA_HEADS // hps, nq),
        in_specs=[
            pl.BlockSpec((MLA_TILE, 2 * LANE * hps), qmap),
            pl.BlockSpec((seq, NOPE_DIM * hps), kvmap),
            pl.BlockSpec((seq, LANE), lambda b, h, i: (b, 0)),
            pl.BlockSpec((seq, V_DIM * hps), kvmap),
        ],
        out_specs=pl.BlockSpec((MLA_TILE, V_DIM * hps), qmap),
        out_shape=jax.ShapeDtypeStruct((batch * seq, MLA_WIDTH), BF16),
        compiler_params=_params(("parallel", "parallel", "arbitrary")),
        name="mla_prompt",
    )(q_cat, k_nope, kr_pad, v)


def _mla_sample_kernel(q_ref, knn_ref, krn_ref, vn_ref, ckv_ref, krc_ref, wuk_ref, wuv_ref, o_ref, *, t_new):
    ckv = ckv_ref[0].astype(BF16)
    k_past = _dot(ckv, wuk_ref[...]).astype(BF16)
    v_past = _dot(ckv, wuv_ref[...]).astype(BF16)
    kr_past = krc_ref[0].astype(BF16)
    kr_new = krn_ref[...]
    for hd in range(MLA_HEADS):
        cols = slice(hd * NOPE_DIM, (hd + 1) * NOPE_DIM)
        q = q_ref[:, 2 * LANE * hd:2 * LANE * (hd + 1)]
        q_nope = q[:, 0:NOPE_DIM]
        q_rope = q[:, NOPE_DIM:NOPE_DIM + ROPE_DIM]
        s_p = (_dot_nt(q_nope, k_past[:, cols]) + _dot_nt(q_rope, kr_past)) * MLA_SCALE
        s_n = _dot_nt(q, jnp.concatenate([knn_ref[:, cols], kr_new], axis=1)) * MLA_SCALE
        m = jnp.maximum(jnp.max(s_p, axis=-1, keepdims=True), jnp.max(s_n, axis=-1, keepdims=True))
        p_p = jnp.exp(s_p - m)
        p_n = jnp.exp(s_n - m)
        l = jnp.sum(p_p, axis=-1, keepdims=True) + jnp.sum(p_n, axis=-1, keepdims=True)
        acc = _dot(p_p.astype(BF16), v_past[:, cols]) + _dot(p_n.astype(BF16), vn_ref[:, cols])
        o_ref[:, cols] = (acc / l).astype(BF16)


def _mla_sample(q_cat, kn_new, kr_new, v_new, cache_ckv, cache_kr, w_uk, w_uv, streams, t_new, past):
    row = lambda b: (b, 0)
    cmap = lambda b: (b, 0, 0)
    const = lambda b: (0, 0)
    return pl.pallas_call(
        functools.partial(_mla_sample_kernel, t_new=t_new),
        grid=(streams,),
        in_specs=[
            pl.BlockSpec((t_new, 2 * LANE * MLA_HEADS), row),
            pl.BlockSpec((t_new, MLA_HEADS * NOPE_DIM), row),
            pl.BlockSpec((t_new, LANE), row),
            pl.BlockSpec((t_new, MLA_WIDTH), row),
            pl.BlockSpec((1, past, KV_LORA), cmap),
            pl.BlockSpec((1, past, ROPE_DIM), cmap),
            pl.BlockSpec((KV_LORA, MLA_HEADS * NOPE_DIM), const),
            pl.BlockSpec((KV_LORA, MLA_WIDTH), const),
        ],
        out_specs=pl.BlockSpec((t_new, MLA_WIDTH), row),
        out_shape=jax.ShapeDtypeStruct((streams * t_new, MLA_WIDTH), BF16),
        compiler_params=_params(("parallel",)),
        name="mla_sample",
    )(q_cat, kn_new, kr_new, v_new, cache_ckv, cache_kr, w_uk, w_uv)


def _merge_ln_kernel(h_ref, osb_ref, omla_ref, wgs_ref, wgm_ref, bgs_ref, bgm_ref, wbs_ref, wbm_ref,
                     wo_ref, g_ref, b_ref, o_ref, hb_ref):
    c = pl.program_id(1)
    last = pl.num_programs(1) - 1

    @pl.when(c == 0)
    def _():
        hb_ref[...] = h_ref[...].astype(BF16)
        o_ref[...] = jnp.zeros_like(o_ref)

    hb = hb_ref[...]
    g_sb = jax.nn.sigmoid(_dot(hb, wgs_ref[...]) + bgs_ref[...])
    g_mla = jax.nn.sigmoid(_dot(hb, wgm_ref[...]) + bgm_ref[...])
    merged = g_sb * _dot(osb_ref[...], wbs_ref[...]) + g_mla * _dot(omla_ref[...], wbm_ref[...])
    o_ref[...] += _dot(merged.astype(BF16), wo_ref[...])

    @pl.when(c == last)
    def _():
        o_ref[...] = _layer_norm(ALPHA * h_ref[...] + o_ref[...], g_ref[...], b_ref[...])


def _merge_ln(h, o_sb, o_mla, w_gs, w_gm, b_gs, b_gm, w_bs, w_bm, w_o, g, b, bm):
    m = h.shape[0]
    nc = D_MODEL // MERGE_CHUNK
    row = lambda i, c: (i, 0)
    col = lambda i, c: (0, c)
    const = lambda i, c: (0, 0)
    return pl.pallas_call(
        _merge_ln_kernel,
        grid=(m // bm, nc),
        in_specs=[
            pl.BlockSpec((bm, D_MODEL), row),
            pl.BlockSpec((bm, SB_WIDTH), row),
            pl.BlockSpec((bm, MLA_WIDTH), row),
            pl.BlockSpec((D_MODEL, MERGE_CHUNK), col),
            pl.BlockSpec((D_MODEL, MERGE_CHUNK), col),
            pl.BlockSpec((1, MERGE_CHUNK), col),
            pl.BlockSpec((1, MERGE_CHUNK), col),
            pl.BlockSpec((SB_WIDTH, MERGE_CHUNK), col),
            pl.BlockSpec((MLA_WIDTH, MERGE_CHUNK), col),
            pl.BlockSpec((MERGE_CHUNK, D_MODEL), lambda i, c: (c, 0)),
            pl.BlockSpec((1, D_MODEL), const),
            pl.BlockSpec((1, D_MODEL), const),
        ],
        out_specs=pl.BlockSpec((bm, D_MODEL), row),
        out_shape=jax.ShapeDtypeStruct((m, D_MODEL), F32),
        scratch_shapes=[pltpu.VMEM((bm, D_MODEL), BF16)],
        compiler_params=_params(("parallel", "arbitrary")),
        name="merge_ln",
    )(h, o_sb, o_mla, w_gs, w_gm, b_gs, b_gm, w_bs, w_bm, w_o, g, b)


def _prep_weights(b_gate, w_uq, w_ukv):
    b_gs = b_gate[:D_MODEL].reshape(1, D_MODEL)
    b_gm = b_gate[D_MODEL:].reshape(1, D_MODEL)
    w_uq3 = w_uq.astype(BF16).reshape(Q_LORA, MLA_HEADS, NOPE_DIM + ROPE_DIM)
    w_uq_p = jnp.pad(w_uq3, ((0, 0), (0, 0), (0, 2 * LANE - NOPE_DIM - ROPE_DIM))).reshape(Q_LORA, 2 * LANE * MLA_HEADS)
    w_ukv3 = w_ukv.astype(BF16).reshape(KV_LORA, MLA_HEADS, NOPE_DIM + V_DIM)
    w_uk = w_ukv3[:, :, :NOPE_DIM].reshape(KV_LORA, MLA_HEADS * NOPE_DIM)
    w_uv = w_ukv3[:, :, NOPE_DIM:].reshape(KV_LORA, MLA_WIDTH)
    return dict(b_gs=b_gs, b_gm=b_gm, w_uq=w_uq_p, w_uk=w_uk, w_uv=w_uv)


def _rope_freq():
    inv_freq = ROPE_THETA ** (-jnp.arange(0, ROPE_DIM, 2, dtype=F32) / ROPE_DIM)
    return jnp.concatenate([inv_freq, inv_freq, jnp.zeros((LANE - ROPE_DIM,), F32)]).reshape(1, LANE)


def _row2(v):
    return v.reshape(1, -1)


def kernel(x_prompt, x_sample, cache_sb_k, cache_sb_v, cache_mla_ckv, cache_mla_krope, ffn1_w_in, ffn1_w_out, ln1_g, ln1_b, w_in, b_gate, g_cq, w_uq, g_ckv, w_ukv, w_br_sb, w_br_mla, w_o, ln2_g, ln2_b, ffn2_w_in, ffn2_w_out, ln3_g, ln3_b):
    assert ffn1_w_in.shape[0] == DEPTH == 1
    b_p, t_p, _ = x_prompt.shape
    b_s, t_s, _ = x_sample.shape
    past = cache_sb_k.shape[2]
    w = _prep_weights(b_gate[0], w_uq[0], w_ukv[0])
    freq = _rope_freq()
    ln1 = (_row2(ln1_g[0]), _row2(ln1_b[0]))
    ln2 = (_row2(ln2_g[0]), _row2(ln2_b[0]))
    ln3 = (_row2(ln3_g[0]), _row2(ln3_b[0]))
    g_cq2, g_ckv2 = _row2(g_cq[0]), _row2(g_ckv[0])
    ffn1 = (ffn1_w_in[0].astype(BF16), ffn1_w_out[0].astype(BF16))

    m_p = b_p * t_p
    later = [(ffn2_w_in[0], 16), (ffn2_w_out[0], 32), (w_in[0], 16), (w_o[0], 16), (w_br_sb[0], 16), (w_br_mla[0], 16)]
    h1p, (ffn2_wi, ffn2_wo, w["w_in"], w["w_o"], w["w_bs"], w["w_bm"]) = _ffn_ln(
        x_prompt.reshape(m_p, D_MODEL), *ffn1, *ln1, ROW_TILE, FF_CHUNK, cast=later)
    ffn2 = (ffn2_wi, ffn2_wo)
    c1 = 3 * SB_WIDTH + Q_LORA + KV_LORA + ROPE_DIM
    w["w_gs"] = w["w_in"][:, c1:c1 + D_MODEL]
    w["w_gm"] = w["w_in"][:, c1 + D_MODEL:]

    def ff_chunk(m):
        return FF_CHUNK if m >= ROW_TILE else FF_CHUNK_FEW_ROWS

    def projections(h1, period, offset):
        m = h1.shape[0]
        q, k, v, kb, vb = _qkv_proj(h1, w["w_in"], min(m, ROW_TILE_SMALL))
        qc, ckv, kr, krb, kn, vm = _lat_proj(h1, w["w_in"], g_cq2, g_ckv2, w["w_uq"], w["w_uk"], w["w_uv"],
                                             freq, min(m, ROW_TILE), period, offset)
        return (q, kb, vb), (qc, kn, krb, vm), (k, v, ckv, kr)

    def rowwise_back(h1, o_sb, o_mla):
        m = h1.shape[0]
        h2 = _merge_ln(h1, o_sb, o_mla, w["w_gs"], w["w_gm"], w["b_gs"], w["b_gm"], w["w_bs"], w["w_bm"],
                       w["w_o"], *ln2, min(m, ROW_TILE))
        return _ffn_ln(h2, *ffn2, *ln3, min(m, ROW_TILE), ff_chunk(m))[0]

    (q, kb, vb), (qc, kn, krb, vm), rows_p = projections(h1p, t_p, 0)
    o_sb = _sb_prompt(q, kb, vb, b_p, t_p)
    o_mla = _mla_prompt(qc, kn, krb, vm, b_p, t_p)
    y_p = rowwise_back(h1p, o_sb, o_mla)

    m_s = b_s * t_s
    h1s = _ffn_ln(x_sample.reshape(m_s, D_MODEL), *ffn1, *ln1, m_s, ff_chunk(m_s))[0]
    (q, kb, vb), (qc, kn, krb, vm), rows_s = projections(h1s, t_s, past)
    o_sb = _sb_sample(q, kb, vb, cache_sb_k.reshape(b_s, past * SB_HEADS, SB_HEAD_DIM),
                      cache_sb_v.reshape(b_s, past * SB_HEADS, SB_HEAD_DIM), b_s, t_s, past)
    o_mla = _mla_sample(qc, kn, krb, vm, cache_mla_ckv[0], cache_mla_krope[0], w["w_uk"], w["w_uv"], b_s, t_s, past)
    y_s = rowwise_back(h1s, o_sb, o_mla)

    def cache_rows(rows, b, t):
        k, v, ckv, kr = rows
        return (k.reshape(1, b, t, SB_HEADS, SB_HEAD_DIM), v.reshape(1, b, t, SB_HEADS, SB_HEAD_DIM),
                ckv.reshape(1, b, t, KV_LORA), kr.reshape(1, b, t, ROPE_DIM))

    return (y_p.reshape(b_p, t_p, D_MODEL), y_s.reshape(b_s, t_s, D_MODEL),
            *cache_rows(rows_p, b_p, t_p), *cache_rows(rows_s, b_s, t_s))
```

```python
import functools

import jax
import jax.numpy as jnp
from jax import lax
from jax.experimental import pallas as pl
from jax.experimental.pallas import tpu as pltpu

D_MODEL = 2048
DEPTH = 1
CHUNK = 64
SB_HEADS = 8
SB_HEAD_DIM = 128
SB_WIDTH = SB_HEADS * SB_HEAD_DIM
MLA_HEADS = 8
Q_LORA = 512
KV_LORA = 512
NOPE_DIM = 128
ROPE_DIM = 64
V_DIM = 128
MLA_WIDTH = MLA_HEADS * V_DIM
ROPE_THETA = 10000.0
D_FF = 5504
LN_EPS = 1e-5
RMS_EPS = 1e-6
ALPHA = (2 * DEPTH) ** 0.25
SB_SCALE = SB_HEAD_DIM ** -0.5
MLA_SCALE = (NOPE_DIM + ROPE_DIM) ** -0.5
MLA_SCALE_LOG2E = MLA_SCALE * 1.4426950408889634

LANE = 128
FF_CHUNK = 512
FF_CHUNK_FEW_ROWS = 1408
MERGE_CHUNK = 512
Q_TILE = 256
MLA_TILE = 512
ATTN_HEADS_PER_STEP = 4
SB_EXP_UNDERFLOW = 120.0
ROW_TILE = 512
ROW_TILE_SMALL = 256
VMEM_LIMIT = 56 * 1024 * 1024

BF16 = jnp.bfloat16
F32 = jnp.float32


def _params(sem):
    return pltpu.CompilerParams(dimension_semantics=sem, vmem_limit_bytes=VMEM_LIMIT)


def _dot(a, b):
    return jnp.dot(a, b, preferred_element_type=F32)


def _dot_nt(a, b):
    return lax.dot_general(a, b, (((1,), (1,)), ((), ())), preferred_element_type=F32)


def _layer_norm(r, g, b):
    mu = jnp.mean(r, axis=-1, keepdims=True)
    d = r - mu
    var = jnp.mean(d * d, axis=-1, keepdims=True)
    return d * lax.rsqrt(var + LN_EPS) * g + b


def _rms_norm(x, g):
    return x * lax.rsqrt(jnp.mean(x * x, axis=-1, keepdims=True) + RMS_EPS) * g


def _ffn_ln_kernel(*refs, overlap, cast_blocks):
    n_cast = len(cast_blocks)
    x_ref, wg_ref, wu_ref, wo_ref, g_ref, b_ref = refs[:6]
    o_ref = refs[6 + n_cast]
    xb_ref = refs[-1]
    j = pl.program_id(1)
    last = pl.num_programs(1) - 1

    step = pl.program_id(0) * pl.num_programs(1) + j
    for k in range(n_cast):
        @pl.when(step < cast_blocks[k])
        def _(src_ref=refs[6 + k], dst_ref=refs[7 + n_cast + k]):
            dst_ref[...] = src_ref[...].astype(BF16)

    @pl.when(j == 0)
    def _():
        xb_ref[...] = x_ref[...].astype(BF16)
        o_ref[...] = jnp.zeros_like(o_ref)

    xb = xb_ref[...]
    gate = _dot(xb, wg_ref[...])
    up = _dot(xb, wu_ref[...])
    act = gate * jax.nn.sigmoid(gate) * up
    col = lax.broadcasted_iota(jnp.int32, act.shape, 1)
    act = jnp.where((j < last) | (col >= overlap), act, 0.0).astype(BF16)
    o_ref[...] += _dot(act, wo_ref[...])

    @pl.when(j == last)
    def _():
        r = ALPHA * x_ref[...] + 0.5 * o_ref[...]
        o_ref[...] = _layer_norm(r, g_ref[...], b_ref[...])


def _ffn_ln(x, w_in, w_out, g, b, bm, chunk, cast=()):
    m = x.shape[0]
    steps = -(-D_FF // chunk)
    overlap = steps * chunk - D_FF
    assert steps > 1 and chunk % LANE == 0 and overlap % LANE == 0 and overlap < chunk
    chunk_start = lambda j, base=0: (base // LANE + j * (chunk // LANE)
                                     - (j // (steps - 1)) * (overlap // LANE)) * LANE
    cast_specs, cast_shapes, cast_blocks = [], [], []
    for arr, rows in cast:
        n_blocks = arr.shape[0] // rows
        assert arr.shape[0] == n_blocks * rows and n_blocks <= (m // bm) * steps and rows % 16 == 0
        cast_specs.append(pl.BlockSpec(
            (rows, arr.shape[1]), lambda i, j, n=n_blocks: (jnp.minimum(i * steps + j, n - 1), 0)))
        cast_shapes.append(jax.ShapeDtypeStruct(arr.shape, BF16))
        cast_blocks.append(n_blocks)
    outs = pl.pallas_call(
        functools.partial(_ffn_ln_kernel, overlap=overlap, cast_blocks=tuple(cast_blocks)),
        grid=(m // bm, steps),
        in_specs=[
            pl.BlockSpec((bm, D_MODEL), lambda i, j: (i, 0)),
            pl.BlockSpec((pl.Element(D_MODEL), pl.Element(chunk)), lambda i, j: (0, chunk_start(j))),
            pl.BlockSpec((pl.Element(D_MODEL), pl.Element(chunk)), lambda i, j: (0, chunk_start(j, D_FF))),
            pl.BlockSpec((pl.Element(chunk), pl.Element(D_MODEL)), lambda i, j: (chunk_start(j), 0)),
            pl.BlockSpec((1, D_MODEL), lambda i, j: (0, 0)),
            pl.BlockSpec((1, D_MODEL), lambda i, j: (0, 0)),
        ] + cast_specs,
        out_specs=[pl.BlockSpec((bm, D_MODEL), lambda i, j: (i, 0))] + cast_specs,
        out_shape=[jax.ShapeDtypeStruct((m, D_MODEL), F32)] + cast_shapes,
        scratch_shapes=[pltpu.VMEM((bm, D_MODEL), BF16)],
        compiler_params=_params(("arbitrary", "arbitrary") if cast else ("parallel", "arbitrary")),
        name="ffn_ln",
    )(x, w_in, w_in, w_out, g, b, *[arr for arr, _ in cast])
    return outs[0], list(outs[1:])


def _qkv_kernel(h_ref, w_ref, q_ref, k_ref, v_ref, kb_ref, vb_ref):
    hb = h_ref[...].astype(BF16)
    q_ref[...] = _dot_nt(hb, w_ref[0:SB_WIDTH, :]).astype(BF16)
    k = _dot_nt(hb, w_ref[SB_WIDTH:2 * SB_WIDTH, :])
    k_ref[...] = k
    kb_ref[...] = k.astype(BF16)
    v = _dot_nt(hb, w_ref[2 * SB_WIDTH:3 * SB_WIDTH, :])
    v_ref[...] = v
    vb_ref[...] = v.astype(BF16)


def _qkv_proj(h, w_qkv, bm):
    m = h.shape[0]
    row = lambda i: (i, 0)
    const = lambda i: (0, 0)
    return pl.pallas_call(
        _qkv_kernel,
        grid=(m // bm,),
        in_specs=[pl.BlockSpec((bm, D_MODEL), row), pl.BlockSpec((3 * SB_WIDTH, D_MODEL), const)],
        out_specs=[pl.BlockSpec((bm, SB_WIDTH), row)] * 5,
        out_shape=[
            jax.ShapeDtypeStruct((m, SB_WIDTH), BF16),
            jax.ShapeDtypeStruct((m, SB_WIDTH), F32),
            jax.ShapeDtypeStruct((m, SB_WIDTH), F32),
            jax.ShapeDtypeStruct((m, SB_WIDTH), BF16),
            jax.ShapeDtypeStruct((m, SB_WIDTH), BF16),
        ],
        compiler_params=_params(("parallel",)),
        name="qkv_proj",
    )(h, w_qkv)


def _rope_lanes(x, cos_t, nsin_lo, sin_hi):
    return x * cos_t + pltpu.roll(x, LANE - ROPE_DIM // 2, 1) * nsin_lo + pltpu.roll(x, ROPE_DIM // 2, 1) * sin_hi


def _lat_kernel(h_ref, wl_ref, gq_ref, gkv_ref, wuq_ref, wuk_ref, wuv_ref, freq_ref,
                qc_ref, ckv_ref, kr_ref, krb_ref, kn_ref, vm_ref, *, bm, period, offset):
    hb = h_ref[...].astype(BF16)
    row = pl.program_id(0) * bm + lax.broadcasted_iota(jnp.int32, (bm, LANE), 0)
    pos = (jnp.bitwise_and(row, period - 1) + offset).astype(F32)
    ang = pos * freq_ref[...]
    lane = lax.broadcasted_iota(jnp.int32, (bm, LANE), 1)
    half = ROPE_DIM // 2
    sin_a = jnp.sin(ang)
    cos_t = jnp.where(lane < ROPE_DIM, jnp.cos(ang), 0.0)
    nsin_lo = jnp.where(lane < half, -sin_a, 0.0)
    sin_hi = jnp.where((lane >= half) & (lane < ROPE_DIM), sin_a, 0.0)

    c_q = _rms_norm(_dot_nt(hb, wl_ref[0:Q_LORA, :]), gq_ref[...]).astype(BF16)
    q_all = _dot(c_q, wuq_ref[...])
    for hd in range(MLA_HEADS):
        base = 2 * LANE * hd
        qc_ref[:, base:base + LANE] = q_all[:, base:base + LANE].astype(BF16)
        qr = _rope_lanes(q_all[:, base + LANE:base + 2 * LANE], cos_t, nsin_lo, sin_hi)
        qc_ref[:, base + LANE:base + 2 * LANE] = qr.astype(BF16)

    c_kv = _rms_norm(_dot_nt(hb, wl_ref[Q_LORA:Q_LORA + KV_LORA, :]), gkv_ref[...])
    ckv_ref[...] = c_kv
    c_kvb = c_kv.astype(BF16)
    kn_ref[...] = _dot(c_kvb, wuk_ref[...]).astype(BF16)
    vm_ref[...] = _dot(c_kvb, wuv_ref[...]).astype(BF16)

    k_r = jnp.where(lane < ROPE_DIM, _dot_nt(hb, wl_ref[Q_LORA + KV_LORA:Q_LORA + KV_LORA + LANE, :]), 0.0)
    k_r = _rope_lanes(k_r, cos_t, nsin_lo, sin_hi)
    kr_ref[...] = k_r[:, 0:ROPE_DIM]
    krb_ref[...] = k_r.astype(BF16)


def _lat_proj(h, w_lat, g_cq, g_ckv, w_uq, w_uk, w_uv, freq, bm, period, offset):
    m = h.shape[0]
    assert period & (period - 1) == 0
    row = lambda i: (i, 0)
    const = lambda i: (0, 0)
    lat_cols = Q_LORA + KV_LORA + LANE
    return pl.pallas_call(
        functools.partial(_lat_kernel, bm=bm, period=period, offset=offset),
        grid=(m // bm,),
        in_specs=[
            pl.BlockSpec((bm, D_MODEL), row),
            pl.BlockSpec((pl.Element(lat_cols), pl.Element(D_MODEL)), lambda i: (3 * SB_WIDTH, 0)),
            pl.BlockSpec((1, Q_LORA), const),
            pl.BlockSpec((1, KV_LORA), const),
            pl.BlockSpec((Q_LORA, 2 * LANE * MLA_HEADS), const),
            pl.BlockSpec((KV_LORA, MLA_HEADS * NOPE_DIM), const),
            pl.BlockSpec((KV_LORA, MLA_WIDTH), const),
            pl.BlockSpec((1, LANE), const),
        ],
        out_specs=[
            pl.BlockSpec((bm, 2 * LANE * MLA_HEADS), row),
            pl.BlockSpec((bm, KV_LORA), row),
            pl.BlockSpec((bm, ROPE_DIM), row),
            pl.BlockSpec((bm, LANE), row),
            pl.BlockSpec((bm, MLA_HEADS * NOPE_DIM), row),
            pl.BlockSpec((bm, MLA_WIDTH), row),
        ],
        out_shape=[
            jax.ShapeDtypeStruct((m, 2 * LANE * MLA_HEADS), BF16),
            jax.ShapeDtypeStruct((m, KV_LORA), F32),
            jax.ShapeDtypeStruct((m, ROPE_DIM), F32),
            jax.ShapeDtypeStruct((m, LANE), BF16),
            jax.ShapeDtypeStruct((m, MLA_HEADS * NOPE_DIM), BF16),
            jax.ShapeDtypeStruct((m, MLA_WIDTH), BF16),
        ],
        compiler_params=_params(("parallel",)),
        name="lat_proj",
    )(h, w_lat, g_cq, g_ckv, w_uq, w_uk, w_uv, freq)


def _strict_upper(n):
    j = lax.broadcasted_iota(jnp.int32, (n, n), 0)
    s = lax.broadcasted_iota(jnp.int32, (n, n), 1)
    return jnp.where(j > s, 1.0, 0.0).astype(BF16)


def _sb_step(q, k, v, u_tri, run, acc, vis):
    tile = u_tri.shape[0]
    z = _dot_nt(q, k) * SB_SCALE
    l1m = -(jnp.maximum(z, 0.0) + jnp.log(1.0 + jnp.exp(-jnp.abs(z))))
    if vis is not None:
        l1m = jnp.where(vis, l1m, 0.0)
    hi = l1m.astype(BF16)
    lo = (l1m - hi.astype(F32)).astype(BF16)
    tails = []
    for blk in reversed(range(k.shape[0] // tile)):
        cols = slice(blk * tile, (blk + 1) * tile)
        tails.insert(0, _dot(hi[:, cols], u_tri) + _dot(lo[:, cols], u_tri) + run)
        run = run + jnp.sum(l1m[:, cols], axis=-1, keepdims=True)
    tail = tails[0] if len(tails) == 1 else jnp.concatenate(tails, axis=1)
    a = jnp.exp(z + l1m + tail)
    if vis is not None:
        a = jnp.where(vis, a, 0.0)
    acc = acc + _dot(a.astype(BF16), v)
    return run, acc


def _sb_prompt_kernel(q_ref, k_ref, v_ref, o_ref, *, tile, heads):
    i = pl.program_id(2)
    u_tri = _strict_upper(tile)
    cols = [slice(g * SB_HEAD_DIM, (g + 1) * SB_HEAD_DIM) for g in range(heads)]
    qs = [q_ref[:, cg] for cg in cols]

    def sweep(s0, width, state, vis):
        return tuple(_sb_step(qs[g], k_ref[pl.ds(s0, width), cols[g]], v_ref[pl.ds(s0, width), cols[g]],
                              u_tri, *state[g], vis) for g in range(heads))

    def live(state):
        return functools.reduce(jnp.maximum, [jnp.max(st[0]) for st in state]) > -SB_EXP_UNDERFLOW

    first = jnp.maximum(i - 1, 0)
    k_pos = first * tile + lax.broadcasted_iota(jnp.int32, (tile, 2 * tile), 1)
    q_pos = i * tile + lax.broadcasted_iota(jnp.int32, (tile, 2 * tile), 0)
    init = tuple((jnp.zeros((tile, 1), F32), jnp.zeros((tile, SB_HEAD_DIM), F32)) for _ in range(heads))
    state = sweep(pl.multiple_of(first * tile, tile), 2 * tile, init, k_pos < q_pos)

    def cond(carry):
        return (carry[0] >= 0) & carry[2]

    def body(carry):
        t, state, _ = carry
        state = sweep(pl.multiple_of(t * tile, tile), tile, state, None)
        return t - 1, state, live(state)

    _, state, _ = lax.while_loop(cond, body, (first - 1, state, live(state)))
    for g in range(heads):
        o_ref[:, cols[g]] = state[g][1].astype(BF16)


def _sb_prompt(q, k, v, batch, seq):
    nq = seq // Q_TILE
    width = ATTN_HEADS_PER_STEP * SB_HEAD_DIM
    qmap = lambda b, h, i: (b * nq + i, h)
    kvmap = lambda b, h, i: (b, h)
    return pl.pallas_call(
        functools.partial(_sb_prompt_kernel, tile=Q_TILE, heads=ATTN_HEADS_PER_STEP),
        grid=(batch, SB_HEADS // ATTN_HEADS_PER_STEP, nq),
        in_specs=[
            pl.BlockSpec((Q_TILE, width), qmap),
            pl.BlockSpec((seq, width), kvmap),
            pl.BlockSpec((seq, width), kvmap),
        ],
        out_specs=pl.BlockSpec((Q_TILE, width), qmap),
        out_shape=jax.ShapeDtypeStruct((batch * seq, SB_WIDTH), BF16),
        compiler_params=_params(("parallel", "parallel", "arbitrary")),
        name="sb_prompt",
    )(q, k, v)


def _sb_sample_kernel(q_ref, kn_ref, vn_ref, kc_ref, vc_ref, o_ref, *, t_new, past):
    n_keys = past + LANE
    s_idx = lax.broadcasted_iota(jnp.int32, (LANE, LANE), 0)
    j_idx = lax.broadcasted_iota(jnp.int32, (LANE, LANE), 1)
    sfx = jnp.where(j_idx > s_idx, 1.0, 0.0).astype(BF16)

    def all_heads(cache_ref, new_ref):
        cached = jnp.concatenate(
            [cache_ref[0, pl.ds(hd, past, stride=SB_HEADS), :] for hd in range(SB_HEADS)], axis=1)
        pad = jnp.zeros((LANE - t_new, SB_WIDTH), BF16)
        return jnp.concatenate([cached.astype(BF16), new_ref[...], pad], axis=0)

    q_rep = jnp.concatenate([q_ref[...]] * SB_HEADS, axis=0)
    row_head = lax.broadcasted_iota(jnp.int32, q_rep.shape, 0) // t_new
    col_head = lax.broadcasted_iota(jnp.int32, q_rep.shape, 1) // SB_HEAD_DIM
    q_bd = jnp.where(row_head == col_head, q_rep, jnp.zeros_like(q_rep))

    z = _dot_nt(all_heads(kc_ref, kn_ref), q_bd) * SB_SCALE
    key = lax.broadcasted_iota(jnp.int32, z.shape, 0)
    query = lax.broadcasted_iota(jnp.int32, z.shape, 1) % t_new
    vis = key < past + query
    l1m = jnp.where(vis, -(jnp.maximum(z, 0.0) + jnp.log(1.0 + jnp.exp(-jnp.abs(z)))), 0.0)
    hi = l1m.astype(BF16)
    lo = (l1m - hi.astype(F32)).astype(BF16)
    tails = []
    right = jnp.zeros((1, z.shape[1]), F32)
    for blk in reversed(range(n_keys // LANE)):
        rows = slice(blk * LANE, (blk + 1) * LANE)
        tails.insert(0, _dot(sfx, hi[rows]) + _dot(sfx, lo[rows]) + right)
        right = right + jnp.sum(l1m[rows], axis=0, keepdims=True)
    a = jnp.where(vis, jnp.exp(z + l1m + jnp.concatenate(tails, axis=0)), 0.0)
    out = _dot(a.T.astype(BF16), all_heads(vc_ref, vn_ref))
    for hd in range(SB_HEADS):
        cols = slice(hd * SB_HEAD_DIM, (hd + 1) * SB_HEAD_DIM)
        o_ref[:, cols] = out[hd * t_new:(hd + 1) * t_new, cols].astype(BF16)


def _sb_sample(q, k_new, v_new, cache_k, cache_v, streams, t_new, past):
    assert SB_HEADS * t_new == LANE and t_new % 16 == 0 and past % LANE == 0
    row = lambda b: (b, 0)
    cmap = lambda b: (b, 0, 0)
    return pl.pallas_call(
        functools.partial(_sb_sample_kernel, t_new=t_new, past=past),
        grid=(streams,),
        in_specs=[
            pl.BlockSpec((t_new, SB_WIDTH), row),
            pl.BlockSpec((t_new, SB_WIDTH), row),
            pl.BlockSpec((t_new, SB_WIDTH), row),
            pl.BlockSpec((1, past * SB_HEADS, SB_HEAD_DIM), cmap),
            pl.BlockSpec((1, past * SB_HEADS, SB_HEAD_DIM), cmap),
        ],
        out_specs=pl.BlockSpec((t_new, SB_WIDTH), row),
        out_shape=jax.ShapeDtypeStruct((streams * t_new, SB_WIDTH), BF16),
        compiler_params=_params(("parallel",)),
        name="sb_sample",
    )(q, k_new, v_new, cache_k, cache_v)


def _softmax_step(q, kc, v, m, l, acc, vis):
    s = _dot_nt(q, kc)
    if vis is not None:
        s = jnp.where(vis, s, -jnp.inf)
    m_new = jnp.maximum(m, jnp.max(s, axis=-1, keepdims=True))
    corr = jnp.exp2((m - m_new) * MLA_SCALE_LOG2E)
    p = jnp.exp2((s - m_new) * MLA_SCALE_LOG2E)
    l = l * corr + jnp.sum(p, axis=-1, keepdims=True)
    acc = acc * corr + _dot(p.astype(BF16), v)
    return m_new, l, acc


def _mla_prompt_kernel(q_ref, kn_ref, kr_ref, v_ref, o_ref, *, tile, heads):
    i = pl.program_id(2)
    qs = [q_ref[:, 2 * LANE * g:2 * LANE * (g + 1)] for g in range(heads)]
    cols = [slice(g * NOPE_DIM, (g + 1) * NOPE_DIM) for g in range(heads)]

    def sweep(s0, state, vis):
        k_rope = kr_ref[pl.ds(s0, tile), :]
        return tuple(_softmax_step(qs[g], jnp.concatenate([kn_ref[pl.ds(s0, tile), cols[g]], k_rope], axis=1),
                                   v_ref[pl.ds(s0, tile), cols[g]], *state[g], vis) for g in range(heads))

    init = tuple((jnp.full((tile, 1), -jnp.inf, F32), jnp.zeros((tile, 1), F32), jnp.zeros((tile, V_DIM), F32))
                 for _ in range(heads))
    state = lax.fori_loop(0, i, lambda t, st: sweep(pl.multiple_of(t * tile, tile), st, None), init)
    r = lax.broadcasted_iota(jnp.int32, (tile, tile), 0)
    c = lax.broadcasted_iota(jnp.int32, (tile, tile), 1)
    state = sweep(pl.multiple_of(i * tile, tile), state, (c // CHUNK) <= (r // CHUNK))
    for g in range(heads):
        _, l, acc = state[g]
        o_ref[:, cols[g]] = (acc / l).astype(BF16)


def _mla_prompt(q_cat, k_nope, kr_pad, v, batch, seq):
    nq = seq // MLA_TILE
    hps = ATTN_HEADS_PER_STEP
    qmap = lambda b, h, i: (b * nq + i, h)
    kvmap = lambda b, h, i: (b, h)
    return pl.pallas_call(
        functools.partial(_mla_prompt_kernel, tile=MLA_TILE, heads=hps),
        grid=(batch, MLA_HEADS // hps, nq),
        in_specs=[
            pl.BlockSpec((MLA_TILE, 2 * LANE * hps), qmap),
            pl.BlockSpec((seq, NOPE_DIM * hps), kvmap),
            pl.BlockSpec((seq, LANE), lambda b, h, i: (b, 0)),
            pl.BlockSpec((seq, V_DIM * hps), kvmap),
        ],
        out_specs=pl.BlockSpec((MLA_TILE, V_DIM * hps), qmap),
        out_shape=jax.ShapeDtypeStruct((batch * seq, MLA_WIDTH), BF16),
        compiler_params=_params(("parallel", "parallel", "arbitrary")),
        name="mla_prompt",
    )(q_cat, k_nope, kr_pad, v)


def _mla_sample_kernel(q_ref, knn_ref, krn_ref, vn_ref, ckv_ref, krc_ref, wuk_ref, wuv_ref, o_ref, *, t_new):
    ckv = ckv_ref[0].astype(BF16)
    k_past = _dot(ckv, wuk_ref[...]).astype(BF16)
    v_past = _dot(ckv, wuv_ref[...]).astype(BF16)
    kr_past = krc_ref[0].astype(BF16)
    kr_new = krn_ref[...]
    for hd in range(MLA_HEADS):
        cols = slice(hd * NOPE_DIM, (hd + 1) * NOPE_DIM)
        q = q_ref[:, 2 * LANE * hd:2 * LANE * (hd + 1)]
        q_nope = q[:, 0:NOPE_DIM]
        q_rope = q[:, NOPE_DIM:NOPE_DIM + ROPE_DIM]
        s_p = (_dot_nt(q_nope, k_past[:, cols]) + _dot_nt(q_rope, kr_past)) * MLA_SCALE
        s_n = _dot_nt(q, jnp.concatenate([knn_ref[:, cols], kr_new], axis=1)) * MLA_SCALE
        m = jnp.maximum(jnp.max(s_p, axis=-1, keepdims=True), jnp.max(s_n, axis=-1, keepdims=True))
        p_p = jnp.exp(s_p - m)
        p_n = jnp.exp(s_n - m)
        l = jnp.sum(p_p, axis=-1, keepdims=True) + jnp.sum(p_n, axis=-1, keepdims=True)
        acc = _dot(p_p.astype(BF16), v_past[:, cols]) + _dot(p_n.astype(BF16), vn_ref[:, cols])
        o_ref[:, cols] = (acc / l).astype(BF16)


def _mla_sample(q_cat, kn_new, kr_new, v_new, cache_ckv, cache_kr, w_uk, w_uv, streams, t_new, past):
    row = lambda b: (b, 0)
    cmap = lambda b: (b, 0, 0)
    const = lambda b: (0, 0)
    return pl.pallas_call(
        functools.partial(_mla_sample_kernel, t_new=t_new),
        grid=(streams,),
        in_specs=[
            pl.BlockSpec((t_new, 2 * LANE * MLA_HEADS), row),
            pl.BlockSpec((t_new, MLA_HEADS * NOPE_DIM), row),
            pl.BlockSpec((t_new, LANE), row),
            pl.BlockSpec((t_new, MLA_WIDTH), row),
            pl.BlockSpec((1, past, KV_LORA), cmap),
            pl.BlockSpec((1, past, ROPE_DIM), cmap),
            pl.BlockSpec((KV_LORA, MLA_HEADS * NOPE_DIM), const),
            pl.BlockSpec((KV_LORA, MLA_WIDTH), const),
        ],
        out_specs=pl.BlockSpec((t_new, MLA_WIDTH), row),
        out_shape=jax.ShapeDtypeStruct((streams * t_new, MLA_WIDTH), BF16),
        compiler_params=_params(("parallel",)),
        name="mla_sample",
    )(q_cat, kn_new, kr_new, v_new, cache_ckv, cache_kr, w_uk, w_uv)


def _merge_ln_kernel(h_ref, osb_ref, omla_ref, wgs_ref, wgm_ref, bgs_ref, bgm_ref, wbs_ref, wbm_ref,
                     wo_ref, g_ref, b_ref, o_ref, hb_ref):
    c = pl.program_id(1)
    last = pl.num_programs(1) - 1

    @pl.when(c == 0)
    def _():
        hb_ref[...] = h_ref[...].astype(BF16)
        o_ref[...] = jnp.zeros_like(o_ref)

    hb = hb_ref[...]
    g_sb = jax.nn.sigmoid(_dot_nt(hb, wgs_ref[...]) + bgs_ref[...])
    g_mla = jax.nn.sigmoid(_dot_nt(hb, wgm_ref[...]) + bgm_ref[...])
    merged = g_sb * _dot(osb_ref[...], wbs_ref[...]) + g_mla * _dot(omla_ref[...], wbm_ref[...])
    o_ref[...] += _dot(merged.astype(BF16), wo_ref[...])

    @pl.when(c == last)
    def _():
        o_ref[...] = _layer_norm(ALPHA * h_ref[...] + o_ref[...], g_ref[...], b_ref[...])


def _merge_ln(h, o_sb, o_mla, w_in_t, b_gs, b_gm, w_bs, w_bm, w_o, g, b, bm):
    m = h.shape[0]
    nc = D_MODEL // MERGE_CHUNK
    row = lambda i, c: (i, 0)
    col = lambda i, c: (0, c)
    const = lambda i, c: (0, 0)
    sub = 16
    gate_row0 = 3 * SB_WIDTH + Q_LORA + KV_LORA + ROPE_DIM
    assert gate_row0 % sub == 0 and MERGE_CHUNK % sub == 0
    gate_rows = lambda base: pl.BlockSpec(
        (pl.Element(MERGE_CHUNK), pl.Element(D_MODEL)),
        lambda i, c: ((base // sub + c * (MERGE_CHUNK // sub)) * sub, 0))
    return pl.pallas_call(
        _merge_ln_kernel,
        grid=(m // bm, nc),
        in_specs=[
            pl.BlockSpec((bm, D_MODEL), row),
            pl.BlockSpec((bm, SB_WIDTH), row),
            pl.BlockSpec((bm, MLA_WIDTH), row),
            gate_rows(gate_row0),
            gate_rows(gate_row0 + D_MODEL),
            pl.BlockSpec((1, MERGE_CHUNK), col),
            pl.BlockSpec((1, MERGE_CHUNK), col),
            pl.BlockSpec((SB_WIDTH, MERGE_CHUNK), col),
            pl.BlockSpec((MLA_WIDTH, MERGE_CHUNK), col),
            pl.BlockSpec((MERGE_CHUNK, D_MODEL), lambda i, c: (c, 0)),
            pl.BlockSpec((1, D_MODEL), const),
            pl.BlockSpec((1, D_MODEL), const),
        ],
        out_specs=pl.BlockSpec((bm, D_MODEL), row),
        out_shape=jax.ShapeDtypeStruct((m, D_MODEL), F32),
        scratch_shapes=[pltpu.VMEM((bm, D_MODEL), BF16)],
        compiler_params=_params(("parallel", "arbitrary")),
        name="merge_ln",
    )(h, o_sb, o_mla, w_in_t, w_in_t, b_gs, b_gm, w_bs, w_bm, w_o, g, b)


def _prep_weights(b_gate, w_uq, w_ukv):
    b_gs = b_gate[:D_MODEL].reshape(1, D_MODEL)
    b_gm = b_gate[D_MODEL:].reshape(1, D_MODEL)
    w_uq3 = w_uq.astype(BF16).reshape(Q_LORA, MLA_HEADS, NOPE_DIM + ROPE_DIM)
    w_uq_p = jnp.pad(w_uq3, ((0, 0), (0, 0), (0, 2 * LANE - NOPE_DIM - ROPE_DIM))).reshape(Q_LORA, 2 * LANE * MLA_HEADS)
    w_ukv3 = w_ukv.astype(BF16).reshape(KV_LORA, MLA_HEADS, NOPE_DIM + V_DIM)
    w_uk = w_ukv3[:, :, :NOPE_DIM].reshape(KV_LORA, MLA_HEADS * NOPE_DIM)
    w_uv = w_ukv3[:, :, NOPE_DIM:].reshape(KV_LORA, MLA_WIDTH)
    return dict(b_gs=b_gs, b_gm=b_gm, w_uq=w_uq_p, w_uk=w_uk, w_uv=w_uv)


def _rope_freq():
    inv_freq = ROPE_THETA ** (-jnp.arange(0, ROPE_DIM, 2, dtype=F32) / ROPE_DIM)
    return jnp.concatenate([inv_freq, inv_freq, jnp.zeros((LANE - ROPE_DIM,), F32)]).reshape(1, LANE)


def _row2(v):
    return v.reshape(1, -1)


def kernel(x_prompt, x_sample, cache_sb_k, cache_sb_v, cache_mla_ckv, cache_mla_krope, ffn1_w_in, ffn1_w_out, ln1_g, ln1_b, w_in, b_gate, g_cq, w_uq, g_ckv, w_ukv, w_br_sb, w_br_mla, w_o, ln2_g, ln2_b, ffn2_w_in, ffn2_w_out, ln3_g, ln3_b):
    assert ffn1_w_in.shape[0] == DEPTH == 1
    b_p, t_p, _ = x_prompt.shape
    b_s, t_s, _ = x_sample.shape
    past = cache_sb_k.shape[2]
    w = _prep_weights(b_gate[0], w_uq[0], w_ukv[0])
    freq = _rope_freq()
    ln1 = (_row2(ln1_g[0]), _row2(ln1_b[0]))
    ln2 = (_row2(ln2_g[0]), _row2(ln2_b[0]))
    ln3 = (_row2(ln3_g[0]), _row2(ln3_b[0]))
    g_cq2, g_ckv2 = _row2(g_cq[0]), _row2(g_ckv[0])
    ffn1 = (ffn1_w_in[0].astype(BF16), ffn1_w_out[0].astype(BF16))

    m_p = b_p * t_p
    later = [(ffn2_w_in[0], 32), (ffn2_w_out[0], 128), (w_in[0].T, 192), (w_o[0], 64),
             (w_br_sb[0], 64), (w_br_mla[0], 64)]
    h1p, (ffn2_wi, ffn2_wo, w["w_in_t"], w["w_o"], w["w_bs"], w["w_bm"]) = _ffn_ln(
        x_prompt.reshape(m_p, D_MODEL), *ffn1, *ln1, ROW_TILE, FF_CHUNK, cast=later)
    ffn2 = (ffn2_wi, ffn2_wo)

    def ff_chunk(m):
        return FF_CHUNK if m >= ROW_TILE else FF_CHUNK_FEW_ROWS

    def projections(h1, period, offset):
        m = h1.shape[0]
        q, k, v, kb, vb = _qkv_proj(h1, w["w_in_t"], min(m, ROW_TILE_SMALL))
        qc, ckv, kr, krb, kn, vm = _lat_proj(h1, w["w_in_t"], g_cq2, g_ckv2, w["w_uq"], w["w_uk"], w["w_uv"],
                                             freq, min(m, ROW_TILE), period, offset)
        return (q, kb, vb), (qc, kn, krb, vm), (k, v, ckv, kr)

    def rowwise_back(h1, o_sb, o_mla):
        m = h1.shape[0]
        h2 = _merge_ln(h1, o_sb, o_mla, w["w_in_t"], w["b_gs"], w["b_gm"], w["w_bs"], w["w_bm"],
                       w["w_o"], *ln2, min(m, ROW_TILE))
        return _ffn_ln(h2, *ffn2, *ln3, min(m, ROW_TILE), ff_chunk(m))[0]

    (q, kb, vb), (qc, kn, krb, vm), rows_p = projections(h1p, t_p, 0)
    o_sb = _sb_prompt(q, kb, vb, b_p, t_p)
    o_mla = _mla_prompt(qc, kn, krb, vm, b_p, t_p)
    y_p = rowwise_back(h1p, o_sb, o_mla)

    m_s = b_s * t_s
    h1s = _ffn_ln(x_sample.reshape(m_s, D_MODEL), *ffn1, *ln1, m_s, ff_chunk(m_s))[0]
    (q, kb, vb), (qc, kn, krb, vm), rows_s = projections(h1s, t_s, past)
    o_sb = _sb_sample(q, kb, vb, cache_sb_k.reshape(b_s, past * SB_HEADS, SB_HEAD_DIM),
                      cache_sb_v.reshape(b_s, past * SB_HEADS, SB_HEAD_DIM), b_s, t_s, past)
    o_mla = _mla_sample(qc, kn, krb, vm, cache_mla_ckv[0], cache_mla_krope[0], w["w_uk"], w["w_uv"], b_s, t_s, past)
    y_s = rowwise_back(h1s, o_sb, o_mla)

    def cache_rows(rows, b, t):
        k, v, ckv, kr = rows
        return (k.reshape(1, b, t, SB_HEADS, SB_HEAD_DIM), v.reshape(1, b, t, SB_HEADS, SB_HEAD_DIM),
                ckv.reshape(1, b, t, KV_LORA), kr.reshape(1, b, t, ROPE_DIM))

    return (y_p.reshape(b_p, t_p, D_MODEL), y_s.reshape(b_s, t_s, D_MODEL),
            *cache_rows(rows_p, b_p, t_p), *cache_rows(rows_s, b_s, t_s))
```

```python
import functools

import jax
import jax.numpy as jnp
from jax import lax
from jax.experimental import pallas as pl
from jax.experimental.pallas import tpu as pltpu

D_MODEL = 2048
DEPTH = 1
CHUNK = 64
SB_HEADS = 8
SB_HEAD_DIM = 128
SB_WIDTH = SB_HEADS * SB_HEAD_DIM
MLA_HEADS = 8
Q_LORA = 512
KV_LORA = 512
NOPE_DIM = 128
ROPE_DIM = 64
V_DIM = 128
MLA_WIDTH = MLA_HEADS * V_DIM
ROPE_THETA = 10000.0
D_FF = 5504
LN_EPS = 1e-5
RMS_EPS = 1e-6
ALPHA = (2 * DEPTH) ** 0.25
SB_SCALE = SB_HEAD_DIM ** -0.5
MLA_SCALE = (NOPE_DIM + ROPE_DIM) ** -0.5
MLA_SCALE_LOG2E = MLA_SCALE * 1.4426950408889634

LANE = 128
FF_CHUNK = 512
FF_CHUNK_FEW_ROWS = 1408
MERGE_CHUNK = 512
Q_TILE = 256
MLA_TILE = 512
ATTN_HEADS_PER_STEP = 4
MLA_HEADS_PER_STEP = 4
SB_EXP_UNDERFLOW = 120.0
ROW_TILE = 512
ROW_TILE_SMALL = 256
VMEM_LIMIT = 56 * 1024 * 1024

BF16 = jnp.bfloat16
F32 = jnp.float32


def _params(sem):
    return pltpu.CompilerParams(dimension_semantics=sem, vmem_limit_bytes=VMEM_LIMIT)


def _dot(a, b):
    return jnp.dot(a, b, preferred_element_type=F32)


def _dot_nt(a, b):
    return lax.dot_general(a, b, (((1,), (1,)), ((), ())), preferred_element_type=F32)


def _layer_norm(r, g, b):
    mu = jnp.mean(r, axis=-1, keepdims=True)
    d = r - mu
    var = jnp.mean(d * d, axis=-1, keepdims=True)
    return d * lax.rsqrt(var + LN_EPS) * g + b


def _rms_norm(x, g):
    return x * lax.rsqrt(jnp.mean(x * x, axis=-1, keepdims=True) + RMS_EPS) * g


def _ffn_ln_kernel(*refs, overlap, cast_blocks):
    n_cast = len(cast_blocks)
    x_ref, wg_ref, wu_ref, wo_ref, g_ref, b_ref = refs[:6]
    o_ref = refs[6 + n_cast]
    xb_ref = refs[-1]
    j = pl.program_id(1)
    last = pl.num_programs(1) - 1

    step = pl.program_id(0) * pl.num_programs(1) + j
    for k in range(n_cast):
        @pl.when(step < cast_blocks[k])
        def _(src_ref=refs[6 + k], dst_ref=refs[7 + n_cast + k]):
            dst_ref[...] = src_ref[...].astype(BF16)

    @pl.when(j == 0)
    def _():
        xb_ref[...] = x_ref[...].astype(BF16)
        o_ref[...] = jnp.zeros_like(o_ref)

    xb = xb_ref[...]
    gate = _dot(xb, wg_ref[...])
    up = _dot(xb, wu_ref[...])
    act = gate * jax.nn.sigmoid(gate) * up
    col = lax.broadcasted_iota(jnp.int32, act.shape, 1)
    act = jnp.where((j < last) | (col >= overlap), act, 0.0).astype(BF16)
    o_ref[...] += _dot(act, wo_ref[...])

    @pl.when(j == last)
    def _():
        r = ALPHA * x_ref[...] + 0.5 * o_ref[...]
        o_ref[...] = _layer_norm(r, g_ref[...], b_ref[...])


def _ffn_ln(x, w_in, w_out, g, b, bm, chunk, cast=()):
    m = x.shape[0]
    steps = -(-D_FF // chunk)
    overlap = steps * chunk - D_FF
    assert steps > 1 and chunk % LANE == 0 and overlap % LANE == 0 and overlap < chunk
    chunk_start = lambda j, base=0: (base // LANE + j * (chunk // LANE)
                                     - (j // (steps - 1)) * (overlap // LANE)) * LANE
    cast_specs, cast_shapes, cast_blocks = [], [], []
    for arr, rows in cast:
        n_blocks = arr.shape[0] // rows
        assert arr.shape[0] == n_blocks * rows and n_blocks <= (m // bm) * steps and rows % 16 == 0
        cast_specs.append(pl.BlockSpec(
            (rows, arr.shape[1]), lambda i, j, n=n_blocks: (jnp.minimum(i * steps + j, n - 1), 0)))
        cast_shapes.append(jax.ShapeDtypeStruct(arr.shape, BF16))
        cast_blocks.append(n_blocks)
    outs = pl.pallas_call(
        functools.partial(_ffn_ln_kernel, overlap=overlap, cast_blocks=tuple(cast_blocks)),
        grid=(m // bm, steps),
        in_specs=[
            pl.BlockSpec((bm, D_MODEL), lambda i, j: (i, 0)),
            pl.BlockSpec((pl.Element(D_MODEL), pl.Element(chunk)), lambda i, j: (0, chunk_start(j))),
            pl.BlockSpec((pl.Element(D_MODEL), pl.Element(chunk)), lambda i, j: (0, chunk_start(j, D_FF))),
            pl.BlockSpec((pl.Element(chunk), pl.Element(D_MODEL)), lambda i, j: (chunk_start(j), 0)),
            pl.BlockSpec((1, D_MODEL), lambda i, j: (0, 0)),
            pl.BlockSpec((1, D_MODEL), lambda i, j: (0, 0)),
        ] + cast_specs,
        out_specs=[pl.BlockSpec((bm, D_MODEL), lambda i, j: (i, 0))] + cast_specs,
        out_shape=[jax.ShapeDtypeStruct((m, D_MODEL), F32)] + cast_shapes,
        scratch_shapes=[pltpu.VMEM((bm, D_MODEL), BF16)],
        compiler_params=_params(("arbitrary", "arbitrary") if cast else ("parallel", "arbitrary")),
        name="ffn_ln",
    )(x, w_in, w_in, w_out, g, b, *[arr for arr, _ in cast])
    return outs[0], list(outs[1:])


def _qkv_kernel(h_ref, w_ref, q_ref, k_ref, v_ref, kb_ref, vb_ref):
    hb = h_ref[...].astype(BF16)
    q_ref[...] = _dot_nt(hb, w_ref[0:SB_WIDTH, :]).astype(BF16)
    k = _dot_nt(hb, w_ref[SB_WIDTH:2 * SB_WIDTH, :])
    k_ref[...] = k
    kb_ref[...] = k.astype(BF16)
    v = _dot_nt(hb, w_ref[2 * SB_WIDTH:3 * SB_WIDTH, :])
    v_ref[...] = v
    vb_ref[...] = v.astype(BF16)


def _qkv_proj(h, w_qkv, bm):
    m = h.shape[0]
    row = lambda i: (i, 0)
    const = lambda i: (0, 0)
    return pl.pallas_call(
        _qkv_kernel,
        grid=(m // bm,),
        in_specs=[pl.BlockSpec((bm, D_MODEL), row), pl.BlockSpec((3 * SB_WIDTH, D_MODEL), const)],
        out_specs=[pl.BlockSpec((bm, SB_WIDTH), row)] * 5,
        out_shape=[
            jax.ShapeDtypeStruct((m, SB_WIDTH), BF16),
            jax.ShapeDtypeStruct((m, SB_WIDTH), F32),
            jax.ShapeDtypeStruct((m, SB_WIDTH), F32),
            jax.ShapeDtypeStruct((m, SB_WIDTH), BF16),
            jax.ShapeDtypeStruct((m, SB_WIDTH), BF16),
        ],
        compiler_params=_params(("parallel",)),
        name="qkv_proj",
    )(h, w_qkv)


def _rope_lanes(x, cos_t, nsin_lo, sin_hi):
    return x * cos_t + pltpu.roll(x, LANE - ROPE_DIM // 2, 1) * nsin_lo + pltpu.roll(x, ROPE_DIM // 2, 1) * sin_hi


def _lat_kernel(h_ref, wl_ref, gq_ref, gkv_ref, wuq_ref, wuk_ref, wuv_ref, freq_ref,
                qc_ref, ckv_ref, kr_ref, krb_ref, kn_ref, vm_ref, *, bm, period, offset):
    hb = h_ref[...].astype(BF16)
    row = pl.program_id(0) * bm + lax.broadcasted_iota(jnp.int32, (bm, LANE), 0)
    pos = (jnp.bitwise_and(row, period - 1) + offset).astype(F32)
    ang = pos * freq_ref[...]
    lane = lax.broadcasted_iota(jnp.int32, (bm, LANE), 1)
    half = ROPE_DIM // 2
    sin_a = jnp.sin(ang)
    cos_t = jnp.where(lane < ROPE_DIM, jnp.cos(ang), 0.0)
    nsin_lo = jnp.where(lane < half, -sin_a, 0.0)
    sin_hi = jnp.where((lane >= half) & (lane < ROPE_DIM), sin_a, 0.0)

    c_q = _rms_norm(_dot_nt(hb, wl_ref[0:Q_LORA, :]), gq_ref[...]).astype(BF16)
    q_all = _dot(c_q, wuq_ref[...]) * MLA_SCALE_LOG2E
    for hd in range(MLA_HEADS):
        base = 2 * LANE * hd
        qc_ref[:, base:base + LANE] = q_all[:, base:base + LANE].astype(BF16)
        qr = _rope_lanes(q_all[:, base + LANE:base + 2 * LANE], cos_t, nsin_lo, sin_hi)
        qc_ref[:, base + LANE:base + 2 * LANE] = qr.astype(BF16)

    c_kv = _rms_norm(_dot_nt(hb, wl_ref[Q_LORA:Q_LORA + KV_LORA, :]), gkv_ref[...])
    ckv_ref[...] = c_kv
    c_kvb = c_kv.astype(BF16)
    kn_ref[...] = _dot(c_kvb, wuk_ref[...]).astype(BF16)
    vm_ref[...] = _dot(c_kvb, wuv_ref[...]).astype(BF16)

    k_r = jnp.where(lane < ROPE_DIM, _dot_nt(hb, wl_ref[Q_LORA + KV_LORA:Q_LORA + KV_LORA + LANE, :]), 0.0)
    k_r = _rope_lanes(k_r, cos_t, nsin_lo, sin_hi)
    kr_ref[...] = k_r[:, 0:ROPE_DIM]
    krb_ref[...] = k_r.astype(BF16)


def _lat_proj(h, w_lat, g_cq, g_ckv, w_uq, w_uk, w_uv, freq, bm, period, offset):
    m = h.shape[0]
    assert period & (period - 1) == 0
    row = lambda i: (i, 0)
    const = lambda i: (0, 0)
    lat_cols = Q_LORA + KV_LORA + LANE
    return pl.pallas_call(
        functools.partial(_lat_kernel, bm=bm, period=period, offset=offset),
        grid=(m // bm,),
        in_specs=[
            pl.BlockSpec((bm, D_MODEL), row),
            pl.BlockSpec((pl.Element(lat_cols), pl.Element(D_MODEL)), lambda i: (3 * SB_WIDTH, 0)),
            pl.BlockSpec((1, Q_LORA), const),
            pl.BlockSpec((1, KV_LORA), const),
            pl.BlockSpec((Q_LORA, 2 * LANE * MLA_HEADS), const),
            pl.BlockSpec((KV_LORA, MLA_HEADS * NOPE_DIM), const),
            pl.BlockSpec((KV_LORA, MLA_WIDTH), const),
            pl.BlockSpec((1, LANE), const),
        ],
        out_specs=[
            pl.BlockSpec((bm, 2 * LANE * MLA_HEADS), row),
            pl.BlockSpec((bm, KV_LORA), row),
            pl.BlockSpec((bm, ROPE_DIM), row),
            pl.BlockSpec((bm, LANE), row),
            pl.BlockSpec((bm, MLA_HEADS * NOPE_DIM), row),
            pl.BlockSpec((bm, MLA_WIDTH), row),
        ],
        out_shape=[
            jax.ShapeDtypeStruct((m, 2 * LANE * MLA_HEADS), BF16),
            jax.ShapeDtypeStruct((m, KV_LORA), F32),
            jax.ShapeDtypeStruct((m, ROPE_DIM), F32),
            jax.ShapeDtypeStruct((m, LANE), BF16),
            jax.ShapeDtypeStruct((m, MLA_HEADS * NOPE_DIM), BF16),
            jax.ShapeDtypeStruct((m, MLA_WIDTH), BF16),
        ],
        compiler_params=_params(("parallel",)),
        name="lat_proj",
    )(h, w_lat, g_cq, g_ckv, w_uq, w_uk, w_uv, freq)


def _strict_upper(n):
    j = lax.broadcasted_iota(jnp.int32, (n, n), 0)
    s = lax.broadcasted_iota(jnp.int32, (n, n), 1)
    return jnp.where(j > s, 1.0, 0.0).astype(BF16)


def _sb_steps(qs, ks, vs, u_tri, state, vis):
    tile = u_tri.shape[0]
    heads = range(len(qs))
    zs = [_dot_nt(qs[g], ks[g]) * SB_SCALE for g in heads]
    l1ms = [-(jnp.maximum(z, 0.0) + jnp.log(1.0 + jnp.exp(-jnp.abs(z)))) for z in zs]
    if vis is not None:
        l1ms = [jnp.where(vis, l, 0.0) for l in l1ms]
    his = [l.astype(BF16) for l in l1ms]
    los = [(l - h.astype(F32)).astype(BF16) for l, h in zip(l1ms, his)]
    runs = [state[g][0] for g in heads]
    tails = [[] for _ in heads]
    for blk in reversed(range(ks[0].shape[0] // tile)):
        cols = slice(blk * tile, (blk + 1) * tile)
        for g in heads:
            tails[g].insert(0, _dot(his[g][:, cols], u_tri) + _dot(los[g][:, cols], u_tri) + runs[g])
            runs[g] = runs[g] + jnp.sum(l1ms[g][:, cols], axis=-1, keepdims=True)
    weights = []
    for g in heads:
        tail = tails[g][0] if len(tails[g]) == 1 else jnp.concatenate(tails[g], axis=1)
        a = jnp.exp(zs[g] + l1ms[g] + tail)
        if vis is not None:
            a = jnp.where(vis, a, 0.0)
        weights.append(a.astype(BF16))
    return tuple((runs[g], state[g][1] + _dot(weights[g], vs[g])) for g in heads)


def _sb_prompt_kernel(q_ref, k_ref, v_ref, o_ref, *, tile, heads):
    i = pl.program_id(2)
    u_tri = _strict_upper(tile)
    cols = [slice(g * SB_HEAD_DIM, (g + 1) * SB_HEAD_DIM) for g in range(heads)]
    qs = [q_ref[:, cg] for cg in cols]

    def sweep(s0, width, state, vis):
        return _sb_steps(qs, [k_ref[pl.ds(s0, width), cg] for cg in cols],
                         [v_ref[pl.ds(s0, width), cg] for cg in cols], u_tri, state, vis)

    def live(state):
        return functools.reduce(jnp.maximum, [jnp.max(st[0]) for st in state]) > -SB_EXP_UNDERFLOW

    first = jnp.maximum(i - 1, 0)
    k_pos = first * tile + lax.broadcasted_iota(jnp.int32, (tile, 2 * tile), 1)
    q_pos = i * tile + lax.broadcasted_iota(jnp.int32, (tile, 2 * tile), 0)
    init = tuple((jnp.zeros((tile, 1), F32), jnp.zeros((tile, SB_HEAD_DIM), F32)) for _ in range(heads))
    state = sweep(pl.multiple_of(first * tile, tile), 2 * tile, init, k_pos < q_pos)

    def cond(carry):
        return (carry[0] >= 0) & carry[2]

    def body(carry):
        t, state, _ = carry
        state = sweep(pl.multiple_of(t * tile, tile), tile, state, None)
        return t - 1, state, live(state)

    _, state, _ = lax.while_loop(cond, body, (first - 1, state, live(state)))
    for g in range(heads):
        o_ref[:, cols[g]] = state[g][1].astype(BF16)


def _sb_prompt(q, k, v, batch, seq):
    nq = seq // Q_TILE
    width = ATTN_HEADS_PER_STEP * SB_HEAD_DIM
    qmap = lambda b, h, i: (b * nq + i, h)
    kvmap = lambda b, h, i: (b, h)
    return pl.pallas_call(
        functools.partial(_sb_prompt_kernel, tile=Q_TILE, heads=ATTN_HEADS_PER_STEP),
        grid=(batch, SB_HEADS // ATTN_HEADS_PER_STEP, nq),
        in_specs=[
            pl.BlockSpec((Q_TILE, width), qmap),
            pl.BlockSpec((seq, width), kvmap),
            pl.BlockSpec((seq, width), kvmap),
        ],
        out_specs=pl.BlockSpec((Q_TILE, width), qmap),
        out_shape=jax.ShapeDtypeStruct((batch * seq, SB_WIDTH), BF16),
        compiler_params=_params(("parallel", "parallel", "arbitrary")),
        name="sb_prompt",
    )(q, k, v)


def _sb_sample_kernel(q_ref, kn_ref, vn_ref, kc_ref, vc_ref, o_ref, *, t_new, past):
    n_keys = past + LANE
    s_idx = lax.broadcasted_iota(jnp.int32, (LANE, LANE), 0)
    j_idx = lax.broadcasted_iota(jnp.int32, (LANE, LANE), 1)
    sfx = jnp.where(j_idx > s_idx, 1.0, 0.0).astype(BF16)

    def all_heads(cache_ref, new_ref):
        cached = jnp.concatenate(
            [cache_ref[0, pl.ds(hd, past, stride=SB_HEADS), :] for hd in range(SB_HEADS)], axis=1)
        pad = jnp.zeros((LANE - t_new, SB_WIDTH), BF16)
        return jnp.concatenate([cached.astype(BF16), new_ref[...], pad], axis=0)

    q_rep = jnp.concatenate([q_ref[...]] * SB_HEADS, axis=0)
    row_head = lax.broadcasted_iota(jnp.int32, q_rep.shape, 0) // t_new
    col_head = lax.broadcasted_iota(jnp.int32, q_rep.shape, 1) // SB_HEAD_DIM
    q_bd = jnp.where(row_head == col_head, q_rep, jnp.zeros_like(q_rep))

    z = _dot_nt(all_heads(kc_ref, kn_ref), q_bd) * SB_SCALE
    key = lax.broadcasted_iota(jnp.int32, z.shape, 0)
    query = lax.broadcasted_iota(jnp.int32, z.shape, 1) % t_new
    vis = key < past + query
    l1m = jnp.where(vis, -(jnp.maximum(z, 0.0) + jnp.log(1.0 + jnp.exp(-jnp.abs(z)))), 0.0)
    hi = l1m.astype(BF16)
    lo = (l1m - hi.astype(F32)).astype(BF16)
    tails = []
    right = jnp.zeros((1, z.shape[1]), F32)
    for blk in reversed(range(n_keys // LANE)):
        rows = slice(blk * LANE, (blk + 1) * LANE)
        tails.insert(0, _dot(sfx, hi[rows]) + _dot(sfx, lo[rows]) + right)
        right = right + jnp.sum(l1m[rows], axis=0, keepdims=True)
    a = jnp.where(vis, jnp.exp(z + l1m + jnp.concatenate(tails, axis=0)), 0.0)
    out = _dot(a.T.astype(BF16), all_heads(vc_ref, vn_ref))
    for hd in range(SB_HEADS):
        cols = slice(hd * SB_HEAD_DIM, (hd + 1) * SB_HEAD_DIM)
        o_ref[:, cols] = out[hd * t_new:(hd + 1) * t_new, cols].astype(BF16)


def _sb_sample(q, k_new, v_new, cache_k, cache_v, streams, t_new, past):
    assert SB_HEADS * t_new == LANE and t_new % 16 == 0 and past % LANE == 0
    row = lambda b: (b, 0)
    cmap = lambda b: (b, 0, 0)
    return pl.pallas_call(
        functools.partial(_sb_sample_kernel, t_new=t_new, past=past),
        grid=(streams,),
        in_specs=[
            pl.BlockSpec((t_new, SB_WIDTH), row),
            pl.BlockSpec((t_new, SB_WIDTH), row),
            pl.BlockSpec((t_new, SB_WIDTH), row),
            pl.BlockSpec((1, past * SB_HEADS, SB_HEAD_DIM), cmap),
            pl.BlockSpec((1, past * SB_HEADS, SB_HEAD_DIM), cmap),
        ],
        out_specs=pl.BlockSpec((t_new, SB_WIDTH), row),
        out_shape=jax.ShapeDtypeStruct((streams * t_new, SB_WIDTH), BF16),
        compiler_params=_params(("parallel",)),
        name="sb_sample",
    )(q, k_new, v_new, cache_k, cache_v)


def _softmax_steps(qs, kcs, v_ones, state, vis):
    scores = [_dot_nt(q, kc) for q, kc in zip(qs, kcs)]
    if vis is not None:
        scores = [jnp.where(vis, s, -jnp.inf) for s in scores]
    m_new = [jnp.maximum(m, jnp.max(s, axis=-1, keepdims=True)) for (m, _), s in zip(state, scores)]
    probs = [jnp.exp2(s - mn).astype(BF16) for s, mn in zip(scores, m_new)]
    return tuple((mn, acc * jnp.exp2(m - mn) + _dot(p, vo))
                 for (m, acc), mn, p, vo in zip(state, m_new, probs, v_ones))


def _mla_prompt_kernel(q_ref, kn_ref, kr_ref, v_ref, o_ref, *, tile, heads):
    i = pl.program_id(2)
    qs = [q_ref[:, 2 * LANE * g:2 * LANE * (g + 1)] for g in range(heads)]
    cols = [slice(g * NOPE_DIM, (g + 1) * NOPE_DIM) for g in range(heads)]
    ones = jnp.ones((tile, V_DIM), BF16)

    def sweep(s0, state, vis):
        k_rope = kr_ref[pl.ds(s0, tile), :]
        kcs = [jnp.concatenate([kn_ref[pl.ds(s0, tile), cg], k_rope], axis=1) for cg in cols]
        v_ones = [jnp.concatenate([v_ref[pl.ds(s0, tile), cg], ones], axis=1) for cg in cols]
        return _softmax_steps(qs, kcs, v_ones, state, vis)

    init = tuple((jnp.full((tile, 1), -jnp.inf, F32), jnp.zeros((tile, 2 * V_DIM), F32)) for _ in range(heads))
    state = lax.fori_loop(0, i, lambda t, st: sweep(pl.multiple_of(t * tile, tile), st, None), init)
    r = lax.broadcasted_iota(jnp.int32, (tile, tile), 0)
    c = lax.broadcasted_iota(jnp.int32, (tile, tile), 1)
    state = sweep(pl.multiple_of(i * tile, tile), state, (c // CHUNK) <= (r // CHUNK))
    for g in range(heads):
        acc = state[g][1]
        o_ref[:, cols[g]] = (acc[:, 0:V_DIM] / acc[:, V_DIM:2 * V_DIM]).astype(BF16)


def _mla_prompt(q_cat, k_nope, kr_pad, v, batch, seq):
    nq = seq // MLA_TILE
    hps = MLA_HEADS_PER_STEP
    qmap = lambda b, h, i: (b * nq + i, h)
    kvmap = lambda b, h, i: (b, h)
    return pl.pallas_call(
        functools.partial(_mla_prompt_kernel, tile=MLA_TILE, heads=hps),
        grid=(batch, MLA_HEADS // hps, nq),
        in_specs=[
            pl.BlockSpec((MLA_TILE, 2 * LANE * hps), qmap),
            pl.BlockSpec((seq, NOPE_DIM * hps), kvmap),
            pl.BlockSpec((seq, LANE), lambda b, h, i: (b, 0)),
            pl.BlockSpec((seq, V_DIM * hps), kvmap),
        ],
        out_specs=pl.BlockSpec((MLA_TILE, V_DIM * hps), qmap),
        out_shape=jax.ShapeDtypeStruct((batch * seq, MLA_WIDTH), BF16),
        compiler_params=_params(("parallel", "parallel", "arbitrary")),
        name="mla_prompt",
    )(q_cat, k_nope, kr_pad, v)


def _mla_sample_kernel(q_ref, knn_ref, krn_ref, vn_ref, ckv_ref, krc_ref, wuk_ref, wuv_ref, o_ref, *, t_new):
    ckv = ckv_ref[0].astype(BF16)
    k_past = _dot(ckv, wuk_ref[...]).astype(BF16)
    v_past = _dot(ckv, wuv_ref[...]).astype(BF16)
    kr_past = krc_ref[0].astype(BF16)
    kr_new = krn_ref[...]
    for hd in range(MLA_HEADS):
        cols = slice(hd * NOPE_DIM, (hd + 1) * NOPE_DIM)
        q = q_ref[:, 2 * LANE * hd:2 * LANE * (hd + 1)]
        q_nope = q[:, 0:NOPE_DIM]
        q_rope = q[:, NOPE_DIM:NOPE_DIM + ROPE_DIM]
        s_p = _dot_nt(q_nope, k_past[:, cols]) + _dot_nt(q_rope, kr_past)
        s_n = _dot_nt(q, jnp.concatenate([knn_ref[:, cols], kr_new], axis=1))
        m = jnp.maximum(jnp.max(s_p, axis=-1, keepdims=True), jnp.max(s_n, axis=-1, keepdims=True))
        p_p = jnp.exp2(s_p - m)
        p_n = jnp.exp2(s_n - m)
        l = jnp.sum(p_p, axis=-1, keepdims=True) + jnp.sum(p_n, axis=-1, keepdims=True)
        acc = _dot(p_p.astype(BF16), v_past[:, cols]) + _dot(p_n.astype(BF16), vn_ref[:, cols])
        o_ref[:, cols] = (acc / l).astype(BF16)


def _mla_sample(q_cat, kn_new, kr_new, v_new, cache_ckv, cache_kr, w_uk, w_uv, streams, t_new, past):
    row = lambda b: (b, 0)
    cmap = lambda b: (b, 0, 0)
    const = lambda b: (0, 0)
    return pl.pallas_call(
        functools.partial(_mla_sample_kernel, t_new=t_new),
        grid=(streams,),
        in_specs=[
            pl.BlockSpec((t_new, 2 * LANE * MLA_HEADS), row),
            pl.BlockSpec((t_new, MLA_HEADS * NOPE_DIM), row),
            pl.BlockSpec((t_new, LANE), row),
            pl.BlockSpec((t_new, MLA_WIDTH), row),
            pl.BlockSpec((1, past, KV_LORA), cmap),
            pl.BlockSpec((1, past, ROPE_DIM), cmap),
            pl.BlockSpec((KV_LORA, MLA_HEADS * NOPE_DIM), const),
            pl.BlockSpec((KV_LORA, MLA_WIDTH), const),
        ],
        out_specs=pl.BlockSpec((t_new, MLA_WIDTH), row),
        out_shape=jax.ShapeDtypeStruct((streams * t_new, MLA_WIDTH), BF16),
        compiler_params=_params(("parallel",)),
        name="mla_sample",
    )(q_cat, kn_new, kr_new, v_new, cache_ckv, cache_kr, w_uk, w_uv)


def _merge_ln_kernel(h_ref, osb_ref, omla_ref, wgs_ref, wgm_ref, bgs_ref, bgm_ref, wbs_ref, wbm_ref,
                     wo_ref, g_ref, b_ref, o_ref, hb_ref):
    c = pl.program_id(1)
    last = pl.num_programs(1) - 1

    @pl.when(c == 0)
    def _():
        hb_ref[...] = h_ref[...].astype(BF16)
        o_ref[...] = jnp.zeros_like(o_ref)

    hb = hb_ref[...]
    g_sb = jax.nn.sigmoid(_dot_nt(hb, wgs_ref[...]) + bgs_ref[...])
    g_mla = jax.nn.sigmoid(_dot_nt(hb, wgm_ref[...]) + bgm_ref[...])
    merged = g_sb * _dot(osb_ref[...], wbs_ref[...]) + g_mla * _dot(omla_ref[...], wbm_ref[...])
    o_ref[...] += _dot(merged.astype(BF16), wo_ref[...])

    @pl.when(c == last)
    def _():
        o_ref[...] = _layer_norm(ALPHA * h_ref[...] + o_ref[...], g_ref[...], b_ref[...])


def _merge_ln(h, o_sb, o_mla, w_in_t, b_gs, b_gm, w_bs, w_bm, w_o, g, b, bm):
    m = h.shape[0]
    nc = D_MODEL // MERGE_CHUNK
    row = lambda i, c: (i, 0)
    col = lambda i, c: (0, c)
    const = lambda i, c: (0, 0)
    sub = 16
    gate_row0 = 3 * SB_WIDTH + Q_LORA + KV_LORA + ROPE_DIM
    assert gate_row0 % sub == 0 and MERGE_CHUNK % sub == 0
    gate_rows = lambda base: pl.BlockSpec(
        (pl.Element(MERGE_CHUNK), pl.Element(D_MODEL)),
        lambda i, c: ((base // sub + c * (MERGE_CHUNK // sub)) * sub, 0))
    return pl.pallas_call(
        _merge_ln_kernel,
        grid=(m // bm, nc),
        in_specs=[
            pl.BlockSpec((bm, D_MODEL), row),
            pl.BlockSpec((bm, SB_WIDTH), row),
            pl.BlockSpec((bm, MLA_WIDTH), row),
            gate_rows(gate_row0),
            gate_rows(gate_row0 + D_MODEL),
            pl.BlockSpec((1, MERGE_CHUNK), col),
            pl.BlockSpec((1, MERGE_CHUNK), col),
            pl.BlockSpec((SB_WIDTH, MERGE_CHUNK), col),
            pl.BlockSpec((MLA_WIDTH, MERGE_CHUNK), col),
            pl.BlockSpec((MERGE_CHUNK, D_MODEL), lambda i, c: (c, 0)),
            pl.BlockSpec((1, D_MODEL), const),
            pl.BlockSpec((1, D_MODEL), const),
        ],
        out_specs=pl.BlockSpec((bm, D_MODEL), row),
        out_shape=jax.ShapeDtypeStruct((m, D_MODEL), F32),
        scratch_shapes=[pltpu.VMEM((bm, D_MODEL), BF16)],
        compiler_params=_params(("parallel", "arbitrary")),
        name="merge_ln",
    )(h, o_sb, o_mla, w_in_t, w_in_t, b_gs, b_gm, w_bs, w_bm, w_o, g, b)


def _prep_weights(b_gate, w_uq, w_ukv):
    b_gs = b_gate[:D_MODEL].reshape(1, D_MODEL)
    b_gm = b_gate[D_MODEL:].reshape(1, D_MODEL)
    w_uq3 = w_uq.astype(BF16).reshape(Q_LORA, MLA_HEADS, NOPE_DIM + ROPE_DIM)
    w_uq_p = jnp.pad(w_uq3, ((0, 0), (0, 0), (0, 2 * LANE - NOPE_DIM - ROPE_DIM))).reshape(Q_LORA, 2 * LANE * MLA_HEADS)
    w_ukv3 = w_ukv.astype(BF16).reshape(KV_LORA, MLA_HEADS, NOPE_DIM + V_DIM)
    w_uk = w_ukv3[:, :, :NOPE_DIM].reshape(KV_LORA, MLA_HEADS * NOPE_DIM)
    w_uv = w_ukv3[:, :, NOPE_DIM:].reshape(KV_LORA, MLA_WIDTH)
    return dict(b_gs=b_gs, b_gm=b_gm, w_uq=w_uq_p, w_uk=w_uk, w_uv=w_uv)


def _rope_freq():
    inv_freq = ROPE_THETA ** (-jnp.arange(0, ROPE_DIM, 2, dtype=F32) / ROPE_DIM)
    return jnp.concatenate([inv_freq, inv_freq, jnp.zeros((LANE - ROPE_DIM,), F32)]).reshape(1, LANE)


def _row2(v):
    return v.reshape(1, -1)


def kernel(x_prompt, x_sample, cache_sb_k, cache_sb_v, cache_mla_ckv, cache_mla_krope, ffn1_w_in, ffn1_w_out, ln1_g, ln1_b, w_in, b_gate, g_cq, w_uq, g_ckv, w_ukv, w_br_sb, w_br_mla, w_o, ln2_g, ln2_b, ffn2_w_in, ffn2_w_out, ln3_g, ln3_b):
    assert ffn1_w_in.shape[0] == DEPTH == 1
    b_p, t_p, _ = x_prompt.shape
    b_s, t_s, _ = x_sample.shape
    past = cache_sb_k.shape[2]
    w = _prep_weights(b_gate[0], w_uq[0], w_ukv[0])
    freq = _rope_freq()
    ln1 = (_row2(ln1_g[0]), _row2(ln1_b[0]))
    ln2 = (_row2(ln2_g[0]), _row2(ln2_b[0]))
    ln3 = (_row2(ln3_g[0]), _row2(ln3_b[0]))
    g_cq2, g_ckv2 = _row2(g_cq[0]), _row2(g_ckv[0])
    ffn1 = (ffn1_w_in[0].astype(BF16), ffn1_w_out[0].astype(BF16))

    m_p = b_p * t_p
    later = [(ffn2_w_in[0], 16), (ffn2_w_out[0], 32), (w_in[0].T, 48), (w_o[0], 16),
             (w_br_sb[0], 16), (w_br_mla[0], 16)]
    h1p, (ffn2_wi, ffn2_wo, w["w_in_t"], w["w_o"], w["w_bs"], w["w_bm"]) = _ffn_ln(
        x_prompt.reshape(m_p, D_MODEL), *ffn1, *ln1, ROW_TILE, FF_CHUNK, cast=later)
    ffn2 = (ffn2_wi, ffn2_wo)

    def ff_chunk(m):
        return FF_CHUNK if m >= ROW_TILE else FF_CHUNK_FEW_ROWS

    def projections(h1, period, offset):
        m = h1.shape[0]
        q, k, v, kb, vb = _qkv_proj(h1, w["w_in_t"], min(m, ROW_TILE_SMALL))
        qc, ckv, kr, krb, kn, vm = _lat_proj(h1, w["w_in_t"], g_cq2, g_ckv2, w["w_uq"], w["w_uk"], w["w_uv"],
                                             freq, min(m, ROW_TILE), period, offset)
        return (q, kb, vb), (qc, kn, krb, vm), (k, v, ckv, kr)

    def rowwise_back(h1, o_sb, o_mla):
        m = h1.shape[0]
        h2 = _merge_ln(h1, o_sb, o_mla, w["w_in_t"], w["b_gs"], w["b_gm"], w["w_bs"], w["w_bm"],
                       w["w_o"], *ln2, min(m, ROW_TILE))
        return _ffn_ln(h2, *ffn2, *ln3, min(m, ROW_TILE), ff_chunk(m))[0]

    (q, kb, vb), (qc, kn, krb, vm), rows_p = projections(h1p, t_p, 0)
    o_sb = _sb_prompt(q, kb, vb, b_p, t_p)
    o_mla = _mla_prompt(qc, kn, krb, vm, b_p, t_p)
    y_p = rowwise_back(h1p, o_sb, o_mla)

    m_s = b_s * t_s
    h1s = _ffn_ln(x_sample.reshape(m_s, D_MODEL), *ffn1, *ln1, m_s, ff_chunk(m_s))[0]
    (q, kb, vb), (qc, kn, krb, vm), rows_s = projections(h1s, t_s, past)
    o_sb = _sb_sample(q, kb, vb, cache_sb_k.reshape(b_s, past * SB_HEADS, SB_HEAD_DIM),
                      cache_sb_v.reshape(b_s, past * SB_HEADS, SB_HEAD_DIM), b_s, t_s, past)
    o_mla = _mla_sample(qc, kn, krb, vm, cache_mla_ckv[0], cache_mla_krope[0], w["w_uk"], w["w_uv"], b_s, t_s, past)
    y_s = rowwise_back(h1s, o_sb, o_mla)

    def cache_rows(rows, b, t):
        k, v, ckv, kr = rows
        return (k.reshape(1, b, t, SB_HEADS, SB_HEAD_DIM), v.reshape(1, b, t, SB_HEADS, SB_HEAD_DIM),
                ckv.reshape(1, b, t, KV_LORA), kr.reshape(1, b, t, ROPE_DIM))

    return (y_p.reshape(b_p, t_p, D_MODEL), y_s.reshape(b_s, t_s, D_MODEL),
            *cache_rows(rows_p, b_p, t_p), *cache_rows(rows_s, b_s, t_s))
```

```python
import functools

import jax
import jax.numpy as jnp
from jax import lax
from jax.experimental import pallas as pl
from jax.experimental.pallas import tpu as pltpu

D_MODEL = 2048
DEPTH = 1
CHUNK = 64
SB_HEADS = 8
SB_HEAD_DIM = 128
SB_WIDTH = SB_HEADS * SB_HEAD_DIM
MLA_HEADS = 8
Q_LORA = 512
KV_LORA = 512
NOPE_DIM = 128
ROPE_DIM = 64
V_DIM = 128
MLA_WIDTH = MLA_HEADS * V_DIM
ROPE_THETA = 10000.0
D_FF = 5504
LN_EPS = 1e-5
RMS_EPS = 1e-6
ALPHA = (2 * DEPTH) ** 0.25
SB_SCALE = SB_HEAD_DIM ** -0.5
MLA_SCALE = (NOPE_DIM + ROPE_DIM) ** -0.5
MLA_SCALE_LOG2E = MLA_SCALE * 1.4426950408889634

LANE = 128
FF_CHUNK = 512
FF_CHUNK_FEW_ROWS = 1408
MERGE_CHUNK = 512
Q_TILE = 256
MLA_TILE = 512
ATTN_HEADS_PER_STEP = 4
MLA_HEADS_PER_STEP = 4
SB_EXP_UNDERFLOW = 120.0
ROW_TILE = 512
ROW_TILE_SMALL = 256
VMEM_LIMIT = 56 * 1024 * 1024

BF16 = jnp.bfloat16
F32 = jnp.float32


def _params(sem):
    return pltpu.CompilerParams(dimension_semantics=sem, vmem_limit_bytes=VMEM_LIMIT)


def _dot(a, b):
    return jnp.dot(a, b, preferred_element_type=F32)


def _dot_nt(a, b):
    return lax.dot_general(a, b, (((1,), (1,)), ((), ())), preferred_element_type=F32)


def _layer_norm(r, g, b):
    mu = jnp.mean(r, axis=-1, keepdims=True)
    d = r - mu
    var = jnp.mean(d * d, axis=-1, keepdims=True)
    return d * lax.rsqrt(var + LN_EPS) * g + b


def _rms_norm(x, g):
    return x * lax.rsqrt(jnp.mean(x * x, axis=-1, keepdims=True) + RMS_EPS) * g


def _ffn_ln_kernel(*refs, overlap, cast_blocks):
    n_cast = len(cast_blocks)
    x_ref, wg_ref, wu_ref, wo_ref, g_ref, b_ref = refs[:6]
    o_ref = refs[6 + n_cast]
    xb_ref = refs[-1]
    j = pl.program_id(1)
    last = pl.num_programs(1) - 1

    step = pl.program_id(0) * pl.num_programs(1) + j
    for k in range(n_cast):
        @pl.when(step < cast_blocks[k])
        def _(src_ref=refs[6 + k], dst_ref=refs[7 + n_cast + k]):
            dst_ref[...] = src_ref[...].astype(BF16)

    @pl.when(j == 0)
    def _():
        xb_ref[...] = x_ref[...].astype(BF16)
        o_ref[...] = jnp.zeros_like(o_ref)

    xb = xb_ref[...]
    gate = _dot(xb, wg_ref[...])
    up = _dot(xb, wu_ref[...])
    act = gate * jax.nn.sigmoid(gate) * up
    col = lax.broadcasted_iota(jnp.int32, act.shape, 1)
    act = jnp.where((j < last) | (col >= overlap), act, 0.0).astype(BF16)
    o_ref[...] += _dot(act, wo_ref[...])

    @pl.when(j == last)
    def _():
        r = ALPHA * x_ref[...] + 0.5 * o_ref[...]
        o_ref[...] = _layer_norm(r, g_ref[...], b_ref[...])


def _ffn_ln(x, w_in, w_out, g, b, bm, chunk, cast=()):
    m = x.shape[0]
    steps = -(-D_FF // chunk)
    overlap = steps * chunk - D_FF
    assert steps > 1 and chunk % LANE == 0 and overlap % LANE == 0 and overlap < chunk
    chunk_start = lambda j, base=0: (base // LANE + j * (chunk // LANE)
                                     - (j // (steps - 1)) * (overlap // LANE)) * LANE
    cast_specs, cast_shapes, cast_blocks = [], [], []
    for arr, rows in cast:
        n_blocks = arr.shape[0] // rows
        assert arr.shape[0] == n_blocks * rows and n_blocks <= (m // bm) * steps and rows % 16 == 0
        cast_specs.append(pl.BlockSpec(
            (rows, arr.shape[1]), lambda i, j, n=n_blocks: (jnp.minimum(i * steps + j, n - 1), 0)))
        cast_shapes.append(jax.ShapeDtypeStruct(arr.shape, BF16))
        cast_blocks.append(n_blocks)
    outs = pl.pallas_call(
        functools.partial(_ffn_ln_kernel, overlap=overlap, cast_blocks=tuple(cast_blocks)),
        grid=(m // bm, steps),
        in_specs=[
            pl.BlockSpec((bm, D_MODEL), lambda i, j: (i, 0)),
            pl.BlockSpec((pl.Element(D_MODEL), pl.Element(chunk)), lambda i, j: (0, chunk_start(j))),
            pl.BlockSpec((pl.Element(D_MODEL), pl.Element(chunk)), lambda i, j: (0, chunk_start(j, D_FF))),
            pl.BlockSpec((pl.Element(chunk), pl.Element(D_MODEL)), lambda i, j: (chunk_start(j), 0)),
            pl.BlockSpec((1, D_MODEL), lambda i, j: (0, 0)),
            pl.BlockSpec((1, D_MODEL), lambda i, j: (0, 0)),
        ] + cast_specs,
        out_specs=[pl.BlockSpec((bm, D_MODEL), lambda i, j: (i, 0))] + cast_specs,
        out_shape=[jax.ShapeDtypeStruct((m, D_MODEL), F32)] + cast_shapes,
        scratch_shapes=[pltpu.VMEM((bm, D_MODEL), BF16)],
        compiler_params=_params(("arbitrary", "arbitrary") if cast else ("parallel", "arbitrary")),
        name="ffn_ln",
    )(x, w_in, w_in, w_out, g, b, *[arr for arr, _ in cast])
    return outs[0], list(outs[1:])


def _qkv_kernel(h_ref, w_ref, q_ref, k_ref, v_ref, kb_ref, vb_ref):
    hb = h_ref[...].astype(BF16)
    q_ref[...] = _dot_nt(hb, w_ref[0:SB_WIDTH, :]).astype(BF16)
    k = _dot_nt(hb, w_ref[SB_WIDTH:2 * SB_WIDTH, :])
    k_ref[...] = k
    kb_ref[...] = k.astype(BF16)
    v = _dot_nt(hb, w_ref[2 * SB_WIDTH:3 * SB_WIDTH, :])
    v_ref[...] = v
    vb_ref[...] = v.astype(BF16)


def _qkv_proj(h, w_qkv, bm):
    m = h.shape[0]
    row = lambda i: (i, 0)
    const = lambda i: (0, 0)
    return pl.pallas_call(
        _qkv_kernel,
        grid=(m // bm,),
        in_specs=[pl.BlockSpec((bm, D_MODEL), row), pl.BlockSpec((3 * SB_WIDTH, D_MODEL), const)],
        out_specs=[pl.BlockSpec((bm, SB_WIDTH), row)] * 5,
        out_shape=[
            jax.ShapeDtypeStruct((m, SB_WIDTH), BF16),
            jax.ShapeDtypeStruct((m, SB_WIDTH), F32),
            jax.ShapeDtypeStruct((m, SB_WIDTH), F32),
            jax.ShapeDtypeStruct((m, SB_WIDTH), BF16),
            jax.ShapeDtypeStruct((m, SB_WIDTH), BF16),
        ],
        compiler_params=_params(("parallel",)),
        name="qkv_proj",
    )(h, w_qkv)


def _rope_lanes(x, cos_t, nsin_lo, sin_hi):
    return x * cos_t + pltpu.roll(x, LANE - ROPE_DIM // 2, 1) * nsin_lo + pltpu.roll(x, ROPE_DIM // 2, 1) * sin_hi


def _lat_kernel(h_ref, wl_ref, gq_ref, gkv_ref, wuq_ref, wuk_ref, wuv_ref, freq_ref,
                qc_ref, ckv_ref, kr_ref, krb_ref, kn_ref, vm_ref, *, bm, period, offset):
    hb = h_ref[...].astype(BF16)
    row = pl.program_id(0) * bm + lax.broadcasted_iota(jnp.int32, (bm, LANE), 0)
    pos = (jnp.bitwise_and(row, period - 1) + offset).astype(F32)
    ang = pos * freq_ref[...]
    lane = lax.broadcasted_iota(jnp.int32, (bm, LANE), 1)
    half = ROPE_DIM // 2
    sin_a = jnp.sin(ang)
    cos_t = jnp.where(lane < ROPE_DIM, jnp.cos(ang), 0.0)
    nsin_lo = jnp.where(lane < half, -sin_a, 0.0)
    sin_hi = jnp.where((lane >= half) & (lane < ROPE_DIM), sin_a, 0.0)

    c_q_raw = _dot_nt(hb, wl_ref[0:Q_LORA, :])
    c_kv_raw = _dot_nt(hb, wl_ref[Q_LORA:Q_LORA + KV_LORA, :])
    k_r = jnp.where(lane < ROPE_DIM, _dot_nt(hb, wl_ref[Q_LORA + KV_LORA:Q_LORA + KV_LORA + LANE, :]), 0.0)
    c_q = _rms_norm(c_q_raw, gq_ref[...]).astype(BF16)
    c_kv = _rms_norm(c_kv_raw, gkv_ref[...])
    c_kvb = c_kv.astype(BF16)
    q_all = _dot(c_q, wuq_ref[...]) * MLA_SCALE_LOG2E
    kn_ref[...] = _dot(c_kvb, wuk_ref[...]).astype(BF16)
    vm_ref[...] = _dot(c_kvb, wuv_ref[...]).astype(BF16)
    ckv_ref[...] = c_kv

    k_r = _rope_lanes(k_r, cos_t, nsin_lo, sin_hi)
    kr_ref[...] = k_r[:, 0:ROPE_DIM]
    krb_ref[...] = k_r.astype(BF16)
    for hd in range(MLA_HEADS):
        base = 2 * LANE * hd
        qc_ref[:, base:base + LANE] = q_all[:, base:base + LANE].astype(BF16)
        qr = _rope_lanes(q_all[:, base + LANE:base + 2 * LANE], cos_t, nsin_lo, sin_hi)
        qc_ref[:, base + LANE:base + 2 * LANE] = qr.astype(BF16)


def _lat_proj(h, w_lat, g_cq, g_ckv, w_uq, w_uk, w_uv, freq, bm, period, offset):
    m = h.shape[0]
    assert period & (period - 1) == 0
    row = lambda i: (i, 0)
    const = lambda i: (0, 0)
    lat_cols = Q_LORA + KV_LORA + LANE
    return pl.pallas_call(
        functools.partial(_lat_kernel, bm=bm, period=period, offset=offset),
        grid=(m // bm,),
        in_specs=[
            pl.BlockSpec((bm, D_MODEL), row),
            pl.BlockSpec((pl.Element(lat_cols), pl.Element(D_MODEL)), lambda i: (3 * SB_WIDTH, 0)),
            pl.BlockSpec((1, Q_LORA), const),
            pl.BlockSpec((1, KV_LORA), const),
            pl.BlockSpec((Q_LORA, 2 * LANE * MLA_HEADS), const),
            pl.BlockSpec((KV_LORA, MLA_HEADS * NOPE_DIM), const),
            pl.BlockSpec((KV_LORA, MLA_WIDTH), const),
            pl.BlockSpec((1, LANE), const),
        ],
        out_specs=[
            pl.BlockSpec((bm, 2 * LANE * MLA_HEADS), row),
            pl.BlockSpec((bm, KV_LORA), row),
            pl.BlockSpec((bm, ROPE_DIM), row),
            pl.BlockSpec((bm, LANE), row),
            pl.BlockSpec((bm, MLA_HEADS * NOPE_DIM), row),
            pl.BlockSpec((bm, MLA_WIDTH), row),
        ],
        out_shape=[
            jax.ShapeDtypeStruct((m, 2 * LANE * MLA_HEADS), BF16),
            jax.ShapeDtypeStruct((m, KV_LORA), F32),
            jax.ShapeDtypeStruct((m, ROPE_DIM), F32),
            jax.ShapeDtypeStruct((m, LANE), BF16),
            jax.ShapeDtypeStruct((m, MLA_HEADS * NOPE_DIM), BF16),
            jax.ShapeDtypeStruct((m, MLA_WIDTH), BF16),
        ],
        compiler_params=_params(("parallel",)),
        name="lat_proj",
    )(h, w_lat, g_cq, g_ckv, w_uq, w_uk, w_uv, freq)


def _strict_upper(n):
    j = lax.broadcasted_iota(jnp.int32, (n, n), 0)
    s = lax.broadcasted_iota(jnp.int32, (n, n), 1)
    return jnp.where(j > s, 1.0, 0.0).astype(BF16)


def _sb_steps(qs, ks, vs, u_tri, state, vis):
    tile = u_tri.shape[0]
    heads = range(len(qs))
    zs = [_dot_nt(qs[g], ks[g]) * SB_SCALE for g in heads]
    l1ms = [-(jnp.maximum(z, 0.0) + jnp.log(1.0 + jnp.exp(-jnp.abs(z)))) for z in zs]
    if vis is not None:
        l1ms = [jnp.where(vis, l, 0.0) for l in l1ms]
    his = [l.astype(BF16) for l in l1ms]
    los = [(l - h.astype(F32)).astype(BF16) for l, h in zip(l1ms, his)]
    runs = [state[g][0] for g in heads]
    tails = [[] for _ in heads]
    for blk in reversed(range(ks[0].shape[0] // tile)):
        cols = slice(blk * tile, (blk + 1) * tile)
        for g in heads:
            tails[g].insert(0, _dot(his[g][:, cols], u_tri) + _dot(los[g][:, cols], u_tri) + runs[g])
            runs[g] = runs[g] + jnp.sum(l1ms[g][:, cols], axis=-1, keepdims=True)
    weights = []
    for g in heads:
        tail = tails[g][0] if len(tails[g]) == 1 else jnp.concatenate(tails[g], axis=1)
        a = jnp.exp(zs[g] + l1ms[g] + tail)
        if vis is not None:
            a = jnp.where(vis, a, 0.0)
        weights.append(a.astype(BF16))
    return tuple((runs[g], state[g][1] + _dot(weights[g], vs[g])) for g in heads)


def _sb_prompt_kernel(q_ref, k_ref, v_ref, o_ref, *, tile, heads):
    i = pl.program_id(2)
    u_tri = _strict_upper(tile)
    cols = [slice(g * SB_HEAD_DIM, (g + 1) * SB_HEAD_DIM) for g in range(heads)]
    qs = [q_ref[:, cg] for cg in cols]

    def sweep(s0, width, state, vis):
        return _sb_steps(qs, [k_ref[pl.ds(s0, width), cg] for cg in cols],
                         [v_ref[pl.ds(s0, width), cg] for cg in cols], u_tri, state, vis)

    def live(state):
        return functools.reduce(jnp.maximum, [jnp.max(st[0]) for st in state]) > -SB_EXP_UNDERFLOW

    first = jnp.maximum(i - 1, 0)
    k_pos = first * tile + lax.broadcasted_iota(jnp.int32, (tile, 2 * tile), 1)
    q_pos = i * tile + lax.broadcasted_iota(jnp.int32, (tile, 2 * tile), 0)
    init = tuple((jnp.zeros((tile, 1), F32), jnp.zeros((tile, SB_HEAD_DIM), F32)) for _ in range(heads))
    state = sweep(pl.multiple_of(first * tile, tile), 2 * tile, init, k_pos < q_pos)

    def cond(carry):
        return (carry[0] >= 0) & carry[2]

    def body(carry):
        t, state, _ = carry
        state = sweep(pl.multiple_of(t * tile, tile), tile, state, None)
        return t - 1, state, live(state)

    _, state, _ = lax.while_loop(cond, body, (first - 1, state, live(state)))
    for g in range(heads):
        o_ref[:, cols[g]] = state[g][1].astype(BF16)


def _sb_prompt(q, k, v, batch, seq):
    nq = seq // Q_TILE
    width = ATTN_HEADS_PER_STEP * SB_HEAD_DIM
    qmap = lambda b, h, i: (b * nq + i, h)
    kvmap = lambda b, h, i: (b, h)
    return pl.pallas_call(
        functools.partial(_sb_prompt_kernel, tile=Q_TILE, heads=ATTN_HEADS_PER_STEP),
        grid=(batch, SB_HEADS // ATTN_HEADS_PER_STEP, nq),
        in_specs=[
            pl.BlockSpec((Q_TILE, width), qmap),
            pl.BlockSpec((seq, width), kvmap),
            pl.BlockSpec((seq, width), kvmap),
        ],
        out_specs=pl.BlockSpec((Q_TILE, width), qmap),
        out_shape=jax.ShapeDtypeStruct((batch * seq, SB_WIDTH), BF16),
        compiler_params=_params(("parallel", "parallel", "arbitrary")),
        name="sb_prompt",
    )(q, k, v)


def _sb_sample_kernel(q_ref, kn_ref, vn_ref, kc_ref, vc_ref, o_ref, *, t_new, past):
    n_keys = past + LANE
    s_idx = lax.broadcasted_iota(jnp.int32, (LANE, LANE), 0)
    j_idx = lax.broadcasted_iota(jnp.int32, (LANE, LANE), 1)
    sfx = jnp.where(j_idx > s_idx, 1.0, 0.0).astype(BF16)

    def all_heads(cache_ref, new_ref):
        cached = jnp.concatenate(
            [cache_ref[0, pl.ds(hd, past, stride=SB_HEADS), :] for hd in range(SB_HEADS)], axis=1)
        pad = jnp.zeros((LANE - t_new, SB_WIDTH), BF16)
        return jnp.concatenate([cached.astype(BF16), new_ref[...], pad], axis=0)

    q_rep = jnp.concatenate([q_ref[...]] * SB_HEADS, axis=0)
    row_head = lax.broadcasted_iota(jnp.int32, q_rep.shape, 0) // t_new
    col_head = lax.broadcasted_iota(jnp.int32, q_rep.shape, 1) // SB_HEAD_DIM
    q_bd = jnp.where(row_head == col_head, q_rep, jnp.zeros_like(q_rep))

    z = _dot_nt(all_heads(kc_ref, kn_ref), q_bd) * SB_SCALE
    key = lax.broadcasted_iota(jnp.int32, z.shape, 0)
    query = lax.broadcasted_iota(jnp.int32, z.shape, 1) % t_new
    vis = key < past + query
    l1m = jnp.where(vis, -(jnp.maximum(z, 0.0) + jnp.log(1.0 + jnp.exp(-jnp.abs(z)))), 0.0)
    hi = l1m.astype(BF16)
    lo = (l1m - hi.astype(F32)).astype(BF16)
    tails = []
    right = jnp.zeros((1, z.shape[1]), F32)
    for blk in reversed(range(n_keys // LANE)):
        rows = slice(blk * LANE, (blk + 1) * LANE)
        tails.insert(0, _dot(sfx, hi[rows]) + _dot(sfx, lo[rows]) + right)
        right = right + jnp.sum(l1m[rows], axis=0, keepdims=True)
    a = jnp.where(vis, jnp.exp(z + l1m + jnp.concatenate(tails, axis=0)), 0.0)
    out = _dot(a.T.astype(BF16), all_heads(vc_ref, vn_ref))
    for hd in range(SB_HEADS):
        cols = slice(hd * SB_HEAD_DIM, (hd + 1) * SB_HEAD_DIM)
        o_ref[:, cols] = out[hd * t_new:(hd + 1) * t_new, cols].astype(BF16)


def _sb_sample(q, k_new, v_new, cache_k, cache_v, streams, t_new, past):
    assert SB_HEADS * t_new == LANE and t_new % 16 == 0 and past % LANE == 0
    row = lambda b: (b, 0)
    cmap = lambda b: (b, 0, 0)
    return pl.pallas_call(
        functools.partial(_sb_sample_kernel, t_new=t_new, past=past),
        grid=(streams,),
        in_specs=[
            pl.BlockSpec((t_new, SB_WIDTH), row),
            pl.BlockSpec((t_new, SB_WIDTH), row),
            pl.BlockSpec((t_new, SB_WIDTH), row),
            pl.BlockSpec((1, past * SB_HEADS, SB_HEAD_DIM), cmap),
            pl.BlockSpec((1, past * SB_HEADS, SB_HEAD_DIM), cmap),
        ],
        out_specs=pl.BlockSpec((t_new, SB_WIDTH), row),
        out_shape=jax.ShapeDtypeStruct((streams * t_new, SB_WIDTH), BF16),
        compiler_params=_params(("parallel",)),
        name="sb_sample",
    )(q, k_new, v_new, cache_k, cache_v)


def _softmax_steps(qs, kcs, v_ones, state, vis):
    scores = [_dot_nt(q, kc) for q, kc in zip(qs, kcs)]
    if vis is not None:
        scores = [jnp.where(vis, s, -jnp.inf) for s in scores]
    m_new = [jnp.maximum(m, jnp.max(s, axis=-1, keepdims=True)) for (m, _), s in zip(state, scores)]
    probs = [jnp.exp2(s - mn).astype(BF16) for s, mn in zip(scores, m_new)]
    return tuple((mn, acc * jnp.exp2(m - mn) + _dot(p, vo))
                 for (m, acc), mn, p, vo in zip(state, m_new, probs, v_ones))


def _mla_prompt_kernel(q_ref, kn_ref, kr_ref, v_ref, o_ref, *, tile, heads):
    i = pl.program_id(2)
    qs = [q_ref[:, 2 * LANE * g:2 * LANE * (g + 1)] for g in range(heads)]
    cols = [slice(g * NOPE_DIM, (g + 1) * NOPE_DIM) for g in range(heads)]
    ones = jnp.ones((tile, V_DIM), BF16)

    def sweep(s0, state, vis):
        k_rope = kr_ref[pl.ds(s0, tile), :]
        kcs = [jnp.concatenate([kn_ref[pl.ds(s0, tile), cg], k_rope], axis=1) for cg in cols]
        v_ones = [jnp.concatenate([v_ref[pl.ds(s0, tile), cg], ones], axis=1) for cg in cols]
        return _softmax_steps(qs, kcs, v_ones, state, vis)

    init = tuple((jnp.full((tile, 1), -jnp.inf, F32), jnp.zeros((tile, 2 * V_DIM), F32)) for _ in range(heads))
    state = lax.fori_loop(0, i, lambda t, st: sweep(pl.multiple_of(t * tile, tile), st, None), init)
    r = lax.broadcasted_iota(jnp.int32, (tile, tile), 0)
    c = lax.broadcasted_iota(jnp.int32, (tile, tile), 1)
    state = sweep(pl.multiple_of(i * tile, tile), state, (c // CHUNK) <= (r // CHUNK))
    for g in range(heads):
        acc = state[g][1]
        o_ref[:, cols[g]] = (acc[:, 0:V_DIM] / acc[:, V_DIM:2 * V_DIM]).astype(BF16)


def _mla_prompt(q_cat, k_nope, kr_pad, v, batch, seq):
    nq = seq // MLA_TILE
    hps = MLA_HEADS_PER_STEP
    qmap = lambda b, h, i: (b * nq + i, h)
    kvmap = lambda b, h, i: (b, h)
    return pl.pallas_call(
        functools.partial(_mla_prompt_kernel, tile=MLA_TILE, heads=hps),
        grid=(batch, MLA_HEADS // hps, nq),
        in_specs=[
            pl.BlockSpec((MLA_TILE, 2 * LANE * hps), qmap),
            pl.BlockSpec((seq, NOPE_DIM * hps), kvmap),
            pl.BlockSpec((seq, LANE), lambda b, h, i: (b, 0)),
            pl.BlockSpec((seq, V_DIM * hps), kvmap),
        ],
        out_specs=pl.BlockSpec((MLA_TILE, V_DIM * hps), qmap),
        out_shape=jax.ShapeDtypeStruct((batch * seq, MLA_WIDTH), BF16),
        compiler_params=_params(("parallel", "parallel", "arbitrary")),
        name="mla_prompt",
    )(q_cat, k_nope, kr_pad, v)


def _mla_sample_kernel(q_ref, knn_ref, krn_ref, vn_ref, ckv_ref, krc_ref, wuk_ref, wuv_ref, o_ref, *, t_new):
    ckv = ckv_ref[0].astype(BF16)
    k_past = _dot(ckv, wuk_ref[...]).astype(BF16)
    v_past = _dot(ckv, wuv_ref[...]).astype(BF16)
    kr_past = krc_ref[0].astype(BF16)
    kr_new = krn_ref[...]
    heads = range(MLA_HEADS)
    cols = [slice(hd * NOPE_DIM, (hd + 1) * NOPE_DIM) for hd in heads]
    qs = [q_ref[:, 2 * LANE * hd:2 * LANE * (hd + 1)] for hd in heads]
    s_past = [_dot_nt(qs[hd][:, 0:NOPE_DIM], k_past[:, cols[hd]])
              + _dot_nt(qs[hd][:, NOPE_DIM:NOPE_DIM + ROPE_DIM], kr_past) for hd in heads]
    s_new = [_dot_nt(qs[hd], jnp.concatenate([knn_ref[:, cols[hd]], kr_new], axis=1)) for hd in heads]
    ms = [jnp.maximum(jnp.max(sp, axis=-1, keepdims=True), jnp.max(sn, axis=-1, keepdims=True))
          for sp, sn in zip(s_past, s_new)]
    p_past = [jnp.exp2(sp - m) for sp, m in zip(s_past, ms)]
    p_new = [jnp.exp2(sn - m) for sn, m in zip(s_new, ms)]
    ls = [jnp.sum(pp, axis=-1, keepdims=True) + jnp.sum(pn, axis=-1, keepdims=True)
          for pp, pn in zip(p_past, p_new)]
    accs = [_dot(p_past[hd].astype(BF16), v_past[:, cols[hd]]) + _dot(p_new[hd].astype(BF16), vn_ref[:, cols[hd]])
            for hd in heads]
    for hd in heads:
        o_ref[:, cols[hd]] = (accs[hd] / ls[hd]).astype(BF16)


def _mla_sample(q_cat, kn_new, kr_new, v_new, cache_ckv, cache_kr, w_uk, w_uv, streams, t_new, past):
    row = lambda b: (b, 0)
    cmap = lambda b: (b, 0, 0)
    const = lambda b: (0, 0)
    return pl.pallas_call(
        functools.partial(_mla_sample_kernel, t_new=t_new),
        grid=(streams,),
        in_specs=[
            pl.BlockSpec((t_new, 2 * LANE * MLA_HEADS), row),
            pl.BlockSpec((t_new, MLA_HEADS * NOPE_DIM), row),
            pl.BlockSpec((t_new, LANE), row),
            pl.BlockSpec((t_new, MLA_WIDTH), row),
            pl.BlockSpec((1, past, KV_LORA), cmap),
            pl.BlockSpec((1, past, ROPE_DIM), cmap),
            pl.BlockSpec((KV_LORA, MLA_HEADS * NOPE_DIM), const),
            pl.BlockSpec((KV_LORA, MLA_WIDTH), const),
        ],
        out_specs=pl.BlockSpec((t_new, MLA_WIDTH), row),
        out_shape=jax.ShapeDtypeStruct((streams * t_new, MLA_WIDTH), BF16),
        compiler_params=_params(("parallel",)),
        name="mla_sample",
    )(q_cat, kn_new, kr_new, v_new, cache_ckv, cache_kr, w_uk, w_uv)


def _merge_ln_kernel(h_ref, osb_ref, omla_ref, wgs_ref, wgm_ref, bgs_ref, bgm_ref, wbs_ref, wbm_ref,
                     wo_ref, g_ref, b_ref, o_ref, hb_ref):
    c = pl.program_id(1)
    last = pl.num_programs(1) - 1

    @pl.when(c == 0)
    def _():
        hb_ref[...] = h_ref[...].astype(BF16)
        o_ref[...] = jnp.zeros_like(o_ref)

    hb = hb_ref[...]
    g_sb = jax.nn.sigmoid(_dot_nt(hb, wgs_ref[...]) + bgs_ref[...])
    g_mla = jax.nn.sigmoid(_dot_nt(hb, wgm_ref[...]) + bgm_ref[...])
    merged = g_sb * _dot(osb_ref[...], wbs_ref[...]) + g_mla * _dot(omla_ref[...], wbm_ref[...])
    o_ref[...] += _dot(merged.astype(BF16), wo_ref[...])

    @pl.when(c == last)
    def _():
        o_ref[...] = _layer_norm(ALPHA * h_ref[...] + o_ref[...], g_ref[...], b_ref[...])


def _merge_ln(h, o_sb, o_mla, w_in_t, b_gs, b_gm, w_bs, w_bm, w_o, g, b, bm):
    m = h.shape[0]
    nc = D_MODEL // MERGE_CHUNK
    row = lambda i, c: (i, 0)
    col = lambda i, c: (0, c)
    const = lambda i, c: (0, 0)
    sub = 16
    gate_row0 = 3 * SB_WIDTH + Q_LORA + KV_LORA + ROPE_DIM
    assert gate_row0 % sub == 0 and MERGE_CHUNK % sub == 0
    gate_rows = lambda base: pl.BlockSpec(
        (pl.Element(MERGE_CHUNK), pl.Element(D_MODEL)),
        lambda i, c: ((base // sub + c * (MERGE_CHUNK // sub)) * sub, 0))
    return pl.pallas_call(
        _merge_ln_kernel,
        grid=(m // bm, nc),
        in_specs=[
            pl.BlockSpec((bm, D_MODEL), row),
            pl.BlockSpec((bm, SB_WIDTH), row),
            pl.BlockSpec((bm, MLA_WIDTH), row),
            gate_rows(gate_row0),
            gate_rows(gate_row0 + D_MODEL),
            pl.BlockSpec((1, MERGE_CHUNK), col),
            pl.BlockSpec((1, MERGE_CHUNK), col),
            pl.BlockSpec((SB_WIDTH, MERGE_CHUNK), col),
            pl.BlockSpec((MLA_WIDTH, MERGE_CHUNK), col),
            pl.BlockSpec((MERGE_CHUNK, D_MODEL), lambda i, c: (c, 0)),
            pl.BlockSpec((1, D_MODEL), const),
            pl.BlockSpec((1, D_MODEL), const),
        ],
        out_specs=pl.BlockSpec((bm, D_MODEL), row),
        out_shape=jax.ShapeDtypeStruct((m, D_MODEL), F32),
        scratch_shapes=[pltpu.VMEM((bm, D_MODEL), BF16)],
        compiler_params=_params(("parallel", "arbitrary")),
        name="merge_ln",
    )(h, o_sb, o_mla, w_in_t, w_in_t, b_gs, b_gm, w_bs, w_bm, w_o, g, b)


def _prep_weights(b_gate, w_uq, w_ukv):
    b_gs = b_gate[:D_MODEL].reshape(1, D_MODEL)
    b_gm = b_gate[D_MODEL:].reshape(1, D_MODEL)
    w_uq3 = w_uq.astype(BF16).reshape(Q_LORA, MLA_HEADS, NOPE_DIM + ROPE_DIM)
    w_uq_p = jnp.pad(w_uq3, ((0, 0), (0, 0), (0, 2 * LANE - NOPE_DIM - ROPE_DIM))).reshape(Q_LORA, 2 * LANE * MLA_HEADS)
    w_ukv3 = w_ukv.astype(BF16).reshape(KV_LORA, MLA_HEADS, NOPE_DIM + V_DIM)
    w_uk = w_ukv3[:, :, :NOPE_DIM].reshape(KV_LORA, MLA_HEADS * NOPE_DIM)
    w_uv = w_ukv3[:, :, NOPE_DIM:].reshape(KV_LORA, MLA_WIDTH)
    return dict(b_gs=b_gs, b_gm=b_gm, w_uq=w_uq_p, w_uk=w_uk, w_uv=w_uv)


def _rope_freq():
    inv_freq = ROPE_THETA ** (-jnp.arange(0, ROPE_DIM, 2, dtype=F32) / ROPE_DIM)
    return jnp.concatenate([inv_freq, inv_freq, jnp.zeros((LANE - ROPE_DIM,), F32)]).reshape(1, LANE)


def _row2(v):
    return v.reshape(1, -1)


def kernel(x_prompt, x_sample, cache_sb_k, cache_sb_v, cache_mla_ckv, cache_mla_krope, ffn1_w_in, ffn1_w_out, ln1_g, ln1_b, w_in, b_gate, g_cq, w_uq, g_ckv, w_ukv, w_br_sb, w_br_mla, w_o, ln2_g, ln2_b, ffn2_w_in, ffn2_w_out, ln3_g, ln3_b):
    assert ffn1_w_in.shape[0] == DEPTH == 1
    b_p, t_p, _ = x_prompt.shape
    b_s, t_s, _ = x_sample.shape
    past = cache_sb_k.shape[2]
    w = _prep_weights(b_gate[0], w_uq[0], w_ukv[0])
    freq = _rope_freq()
    ln1 = (_row2(ln1_g[0]), _row2(ln1_b[0]))
    ln2 = (_row2(ln2_g[0]), _row2(ln2_b[0]))
    ln3 = (_row2(ln3_g[0]), _row2(ln3_b[0]))
    g_cq2, g_ckv2 = _row2(g_cq[0]), _row2(g_ckv[0])
    ffn1 = (ffn1_w_in[0].astype(BF16), ffn1_w_out[0].astype(BF16))

    m_p = b_p * t_p
    later = [(ffn2_w_in[0], 16), (ffn2_w_out[0], 32), (w_in[0].T, 48), (w_o[0], 16),
             (w_br_sb[0], 16), (w_br_mla[0], 16)]
    h1p, (ffn2_wi, ffn2_wo, w["w_in_t"], w["w_o"], w["w_bs"], w["w_bm"]) = _ffn_ln(
        x_prompt.reshape(m_p, D_MODEL), *ffn1, *ln1, ROW_TILE, FF_CHUNK, cast=later)
    ffn2 = (ffn2_wi, ffn2_wo)

    def ff_chunk(m):
        return FF_CHUNK if m >= ROW_TILE else FF_CHUNK_FEW_ROWS

    def projections(h1, period, offset):
        m = h1.shape[0]
        q, k, v, kb, vb = _qkv_proj(h1, w["w_in_t"], min(m, ROW_TILE_SMALL))
        qc, ckv, kr, krb, kn, vm = _lat_proj(h1, w["w_in_t"], g_cq2, g_ckv2, w["w_uq"], w["w_uk"], w["w_uv"],
                                             freq, min(m, ROW_TILE), period, offset)
        return (q, kb, vb), (qc, kn, krb, vm), (k, v, ckv, kr)

    def rowwise_back(h1, o_sb, o_mla):
        m = h1.shape[0]
        h2 = _merge_ln(h1, o_sb, o_mla, w["w_in_t"], w["b_gs"], w["b_gm"], w["w_bs"], w["w_bm"],
                       w["w_o"], *ln2, min(m, ROW_TILE))
        return _ffn_ln(h2, *ffn2, *ln3, min(m, ROW_TILE), ff_chunk(m))[0]

    (q, kb, vb), (qc, kn, krb, vm), rows_p = projections(h1p, t_p, 0)
    o_sb = _sb_prompt(q, kb, vb, b_p, t_p)
    o_mla = _mla_prompt(qc, kn, krb, vm, b_p, t_p)
    y_p = rowwise_back(h1p, o_sb, o_mla)

    m_s = b_s * t_s
    h1s = _ffn_ln(x_sample.reshape(m_s, D_MODEL), *ffn1, *ln1, m_s, ff_chunk(m_s))[0]
    (q, kb, vb), (qc, kn, krb, vm), rows_s = projections(h1s, t_s, past)
    o_sb = _sb_sample(q, kb, vb, cache_sb_k.reshape(b_s, past * SB_HEADS, SB_HEAD_DIM),
                      cache_sb_v.reshape(b_s, past * SB_HEADS, SB_HEAD_DIM), b_s, t_s, past)
    o_mla = _mla_sample(qc, kn, krb, vm, cache_mla_ckv[0], cache_mla_krope[0], w["w_uk"], w["w_uv"], b_s, t_s, past)
    y_s = rowwise_back(h1s, o_sb, o_mla)

    def cache_rows(rows, b, t):
        k, v, ckv, kr = rows
        return (k.reshape(1, b, t, SB_HEADS, SB_HEAD_DIM), v.reshape(1, b, t, SB_HEADS, SB_HEAD_DIM),
                ckv.reshape(1, b, t, KV_LORA), kr.reshape(1, b, t, ROPE_DIM))

    return (y_p.reshape(b_p, t_p, D_MODEL), y_s.reshape(b_s, t_s, D_MODEL),
            *cache_rows(rows_p, b_p, t_p), *cache_rows(rows_s, b_s, t_s))
```

```python
import functools

import jax
import jax.numpy as jnp
from jax import lax
from jax.experimental import pallas as pl
from jax.experimental.pallas import tpu as pltpu

D_MODEL = 2048
DEPTH = 1
CHUNK = 64
SB_HEADS = 8
SB_HEAD_DIM = 128
SB_WIDTH = SB_HEADS * SB_HEAD_DIM
MLA_HEADS = 8
Q_LORA = 512
KV_LORA = 512
NOPE_DIM = 128
ROPE_DIM = 64
V_DIM = 128
MLA_WIDTH = MLA_HEADS * V_DIM
ROPE_THETA = 10000.0
D_FF = 5504
LN_EPS = 1e-5
RMS_EPS = 1e-6
ALPHA = (2 * DEPTH) ** 0.25
SB_SCALE = SB_HEAD_DIM ** -0.5
LOG2E = 1.4426950408889634
SB_SCALE_LOG2E = SB_SCALE * LOG2E
MLA_SCALE = (NOPE_DIM + ROPE_DIM) ** -0.5
MLA_SCALE_LOG2E = MLA_SCALE * LOG2E

LANE = 128
FF_CHUNK = 512
FF_CHUNK_FEW_ROWS = 1408
MERGE_CHUNK = 512
Q_TILE = 256
MLA_TILE = 512
ATTN_HEADS_PER_STEP = 8
MLA_HEADS_PER_STEP = 4
SB_EXP2_UNDERFLOW = 170.0
ROW_TILE = 512
ROW_TILE_SMALL = 256
VMEM_LIMIT = 56 * 1024 * 1024

BF16 = jnp.bfloat16
F32 = jnp.float32


def _params(sem):
    return pltpu.CompilerParams(dimension_semantics=sem, vmem_limit_bytes=VMEM_LIMIT)


def _dot(a, b):
    return jnp.dot(a, b, preferred_element_type=F32)


def _dot_nt(a, b):
    return lax.dot_general(a, b, (((1,), (1,)), ((), ())), preferred_element_type=F32)


def _layer_norm(r, g, b):
    mu = jnp.mean(r, axis=-1, keepdims=True)
    d = r - mu
    var = jnp.mean(d * d, axis=-1, keepdims=True)
    return d * lax.rsqrt(var + LN_EPS) * g + b


def _rms_norm(x, g):
    return x * lax.rsqrt(jnp.mean(x * x, axis=-1, keepdims=True) + RMS_EPS) * g


def _ffn_ln_kernel(*refs, overlap, cast_blocks):
    n_cast = len(cast_blocks)
    x_ref, wg_ref, wu_ref, wo_ref, g_ref, b_ref = refs[:6]
    o_ref = refs[6 + n_cast]
    xb_ref = refs[-1]
    j = pl.program_id(1)
    last = pl.num_programs(1) - 1

    step = pl.program_id(0) * pl.num_programs(1) + j
    for k in range(n_cast):
        @pl.when(step < cast_blocks[k])
        def _(src_ref=refs[6 + k], dst_ref=refs[7 + n_cast + k]):
            dst_ref[...] = src_ref[...].astype(BF16)

    @pl.when(j == 0)
    def _():
        xb_ref[...] = x_ref[...].astype(BF16)
        o_ref[...] = jnp.zeros_like(o_ref)

    xb = xb_ref[...]
    gate = _dot(xb, wg_ref[...])
    up = _dot(xb, wu_ref[...])
    act = gate * jax.nn.sigmoid(gate) * up
    col = lax.broadcasted_iota(jnp.int32, act.shape, 1)
    act = jnp.where((j < last) | (col >= overlap), act, 0.0).astype(BF16)
    o_ref[...] += _dot(act, wo_ref[...])

    @pl.when(j == last)
    def _():
        r = ALPHA * x_ref[...] + 0.5 * o_ref[...]
        o_ref[...] = _layer_norm(r, g_ref[...], b_ref[...])


def _ffn_ln(x, w_in, w_out, g, b, bm, chunk, cast=()):
    m = x.shape[0]
    steps = -(-D_FF // chunk)
    overlap = steps * chunk - D_FF
    assert steps > 1 and chunk % LANE == 0 and overlap % LANE == 0 and overlap < chunk
    chunk_start = lambda j, base=0: (base // LANE + j * (chunk // LANE)
                                     - (j // (steps - 1)) * (overlap // LANE)) * LANE
    cast_specs, cast_shapes, cast_blocks = [], [], []
    for arr, rows in cast:
        n_blocks = arr.shape[0] // rows
        assert arr.shape[0] == n_blocks * rows and n_blocks <= (m // bm) * steps and rows % 16 == 0
        cast_specs.append(pl.BlockSpec(
            (rows, arr.shape[1]), lambda i, j, n=n_blocks: (jnp.minimum(i * steps + j, n - 1), 0)))
        cast_shapes.append(jax.ShapeDtypeStruct(arr.shape, BF16))
        cast_blocks.append(n_blocks)
    outs = pl.pallas_call(
        functools.partial(_ffn_ln_kernel, overlap=overlap, cast_blocks=tuple(cast_blocks)),
        grid=(m // bm, steps),
        in_specs=[
            pl.BlockSpec((bm, D_MODEL), lambda i, j: (i, 0)),
            pl.BlockSpec((pl.Element(D_MODEL), pl.Element(chunk)), lambda i, j: (0, chunk_start(j))),
            pl.BlockSpec((pl.Element(D_MODEL), pl.Element(chunk)), lambda i, j: (0, chunk_start(j, D_FF))),
            pl.BlockSpec((pl.Element(chunk), pl.Element(D_MODEL)), lambda i, j: (chunk_start(j), 0)),
            pl.BlockSpec((1, D_MODEL), lambda i, j: (0, 0)),
            pl.BlockSpec((1, D_MODEL), lambda i, j: (0, 0)),
        ] + cast_specs,
        out_specs=[pl.BlockSpec((bm, D_MODEL), lambda i, j: (i, 0))] + cast_specs,
        out_shape=[jax.ShapeDtypeStruct((m, D_MODEL), F32)] + cast_shapes,
        scratch_shapes=[pltpu.VMEM((bm, D_MODEL), BF16)],
        compiler_params=_params(("arbitrary", "arbitrary") if cast else ("parallel", "arbitrary")),
        name="ffn_ln",
    )(x, w_in, w_in, w_out, g, b, *[arr for arr, _ in cast])
    return outs[0], list(outs[1:])


def _qkv_kernel(h_ref, w_ref, q_ref, k_ref, v_ref, kb_ref, vb_ref):
    hb = h_ref[...].astype(BF16)
    q_ref[...] = (_dot_nt(hb, w_ref[0:SB_WIDTH, :]) * SB_SCALE_LOG2E).astype(BF16)
    k = _dot_nt(hb, w_ref[SB_WIDTH:2 * SB_WIDTH, :])
    k_ref[...] = k
    kb_ref[...] = k.astype(BF16)
    v = _dot_nt(hb, w_ref[2 * SB_WIDTH:3 * SB_WIDTH, :])
    v_ref[...] = v
    vb_ref[...] = v.astype(BF16)


def _qkv_proj(h, w_qkv, bm):
    m = h.shape[0]
    row = lambda i: (i, 0)
    const = lambda i: (0, 0)
    return pl.pallas_call(
        _qkv_kernel,
        grid=(m // bm,),
        in_specs=[pl.BlockSpec((bm, D_MODEL), row), pl.BlockSpec((3 * SB_WIDTH, D_MODEL), const)],
        out_specs=[pl.BlockSpec((bm, SB_WIDTH), row)] * 5,
        out_shape=[
            jax.ShapeDtypeStruct((m, SB_WIDTH), BF16),
            jax.ShapeDtypeStruct((m, SB_WIDTH), F32),
            jax.ShapeDtypeStruct((m, SB_WIDTH), F32),
            jax.ShapeDtypeStruct((m, SB_WIDTH), BF16),
            jax.ShapeDtypeStruct((m, SB_WIDTH), BF16),
        ],
        compiler_params=_params(("parallel",)),
        name="qkv_proj",
    )(h, w_qkv)


def _rope_lanes(x, cos_t, nsin_lo, sin_hi):
    return x * cos_t + pltpu.roll(x, LANE - ROPE_DIM // 2, 1) * nsin_lo + pltpu.roll(x, ROPE_DIM // 2, 1) * sin_hi


def _lat_kernel(h_ref, wl_ref, gq_ref, gkv_ref, wuq_ref, wuk_ref, wuv_ref, freq_ref,
                qc_ref, ckv_ref, kr_ref, krb_ref, kn_ref, vm_ref, *, bm, period, offset):
    hb = h_ref[...].astype(BF16)
    row = pl.program_id(0) * bm + lax.broadcasted_iota(jnp.int32, (bm, LANE), 0)
    pos = (jnp.bitwise_and(row, period - 1) + offset).astype(F32)
    ang = pos * freq_ref[...]
    lane = lax.broadcasted_iota(jnp.int32, (bm, LANE), 1)
    half = ROPE_DIM // 2
    sin_a = jnp.sin(ang)
    cos_t = jnp.where(lane < ROPE_DIM, jnp.cos(ang), 0.0)
    nsin_lo = jnp.where(lane < half, -sin_a, 0.0)
    sin_hi = jnp.where((lane >= half) & (lane < ROPE_DIM), sin_a, 0.0)

    c_q_raw = _dot_nt(hb, wl_ref[0:Q_LORA, :])
    c_kv_raw = _dot_nt(hb, wl_ref[Q_LORA:Q_LORA + KV_LORA, :])
    k_r = jnp.where(lane < ROPE_DIM, _dot_nt(hb, wl_ref[Q_LORA + KV_LORA:Q_LORA + KV_LORA + LANE, :]), 0.0)
    c_q = _rms_norm(c_q_raw, gq_ref[...]).astype(BF16)
    c_kv = _rms_norm(c_kv_raw, gkv_ref[...])
    c_kvb = c_kv.astype(BF16)
    q_all = _dot(c_q, wuq_ref[...]) * MLA_SCALE_LOG2E
    kn_ref[...] = _dot(c_kvb, wuk_ref[...]).astype(BF16)
    vm_ref[...] = _dot(c_kvb, wuv_ref[...]).astype(BF16)
    ckv_ref[...] = c_kv

    k_r = _rope_lanes(k_r, cos_t, nsin_lo, sin_hi)
    kr_ref[...] = k_r[:, 0:ROPE_DIM]
    krb_ref[...] = k_r.astype(BF16)
    for hd in range(MLA_HEADS):
        base = 2 * LANE * hd
        qc_ref[:, base:base + LANE] = q_all[:, base:base + LANE].astype(BF16)
        qr = _rope_lanes(q_all[:, base + LANE:base + 2 * LANE], cos_t, nsin_lo, sin_hi)
        qc_ref[:, base + LANE:base + 2 * LANE] = qr.astype(BF16)


def _lat_proj(h, w_lat, g_cq, g_ckv, w_uq, w_uk, w_uv, freq, bm, period, offset):
    m = h.shape[0]
    assert period & (period - 1) == 0
    row = lambda i: (i, 0)
    const = lambda i: (0, 0)
    lat_cols = Q_LORA + KV_LORA + LANE
    return pl.pallas_call(
        functools.partial(_lat_kernel, bm=bm, period=period, offset=offset),
        grid=(m // bm,),
        in_specs=[
            pl.BlockSpec((bm, D_MODEL), row),
            pl.BlockSpec((pl.Element(lat_cols), pl.Element(D_MODEL)), lambda i: (3 * SB_WIDTH, 0)),
            pl.BlockSpec((1, Q_LORA), const),
            pl.BlockSpec((1, KV_LORA), const),
            pl.BlockSpec((Q_LORA, 2 * LANE * MLA_HEADS), const),
            pl.BlockSpec((KV_LORA, MLA_HEADS * NOPE_DIM), const),
            pl.BlockSpec((KV_LORA, MLA_WIDTH), const),
            pl.BlockSpec((1, LANE), const),
        ],
        out_specs=[
            pl.BlockSpec((bm, 2 * LANE * MLA_HEADS), row),
            pl.BlockSpec((bm, KV_LORA), row),
            pl.BlockSpec((bm, ROPE_DIM), row),
            pl.BlockSpec((bm, LANE), row),
            pl.BlockSpec((bm, MLA_HEADS * NOPE_DIM), row),
            pl.BlockSpec((bm, MLA_WIDTH), row),
        ],
        out_shape=[
            jax.ShapeDtypeStruct((m, 2 * LANE * MLA_HEADS), BF16),
            jax.ShapeDtypeStruct((m, KV_LORA), F32),
            jax.ShapeDtypeStruct((m, ROPE_DIM), F32),
            jax.ShapeDtypeStruct((m, LANE), BF16),
            jax.ShapeDtypeStruct((m, MLA_HEADS * NOPE_DIM), BF16),
            jax.ShapeDtypeStruct((m, MLA_WIDTH), BF16),
        ],
        compiler_params=_params(("parallel",)),
        name="lat_proj",
    )(h, w_lat, g_cq, g_ckv, w_uq, w_uk, w_uv, freq)


def _strict_upper(n):
    j = lax.broadcasted_iota(jnp.int32, (n, n), 0)
    s = lax.broadcasted_iota(jnp.int32, (n, n), 1)
    return jnp.where(j > s, 1.0, 0.0).astype(BF16)


def _log2_one_minus_beta(z2):
    return -(jnp.maximum(z2, 0.0) + jnp.log2(1.0 + jnp.exp2(-jnp.abs(z2))))


def _sb_steps(qs, ks, vs, u_tri, state, vis):
    tile = u_tri.shape[0]
    heads = range(len(qs))
    zs = [_dot_nt(qs[g], ks[g]) for g in heads]
    l1ms = [_log2_one_minus_beta(z) for z in zs]
    if vis is not None:
        l1ms = [jnp.where(vis, l, 0.0) for l in l1ms]
    his = [l.astype(BF16) for l in l1ms]
    los = [(l - h.astype(F32)).astype(BF16) for l, h in zip(l1ms, his)]
    runs = [state[g][0] for g in heads]
    tails = [[] for _ in heads]
    for blk in reversed(range(ks[0].shape[0] // tile)):
        cols = slice(blk * tile, (blk + 1) * tile)
        for g in heads:
            tails[g].insert(0, _dot(his[g][:, cols], u_tri) + _dot(los[g][:, cols], u_tri) + runs[g])
            runs[g] = runs[g] + jnp.sum(l1ms[g][:, cols], axis=-1, keepdims=True)
    weights = []
    for g in heads:
        tail = tails[g][0] if len(tails[g]) == 1 else jnp.concatenate(tails[g], axis=1)
        a = jnp.exp2(zs[g] + l1ms[g] + tail)
        if vis is not None:
            a = jnp.where(vis, a, 0.0)
        weights.append(a.astype(BF16))
    return tuple((runs[g], state[g][1] + _dot(weights[g], vs[g])) for g in heads)


def _sb_prompt_kernel(q_ref, k_ref, v_ref, o_ref, *, tile, heads):
    i = pl.program_id(2)
    u_tri = _strict_upper(tile)
    cols = [slice(g * SB_HEAD_DIM, (g + 1) * SB_HEAD_DIM) for g in range(heads)]
    qs = [q_ref[:, cg] for cg in cols]

    def sweep(s0, width, state, vis):
        return _sb_steps(qs, [k_ref[pl.ds(s0, width), cg] for cg in cols],
                         [v_ref[pl.ds(s0, width), cg] for cg in cols], u_tri, state, vis)

    def live(state):
        return functools.reduce(jnp.maximum, [jnp.max(st[0]) for st in state]) > -SB_EXP2_UNDERFLOW

    first = jnp.maximum(i - 1, 0)
    k_pos = first * tile + lax.broadcasted_iota(jnp.int32, (tile, 2 * tile), 1)
    q_pos = i * tile + lax.broadcasted_iota(jnp.int32, (tile, 2 * tile), 0)
    init = tuple((jnp.zeros((tile, 1), F32), jnp.zeros((tile, SB_HEAD_DIM), F32)) for _ in range(heads))
    state = sweep(pl.multiple_of(first * tile, tile), 2 * tile, init, k_pos < q_pos)

    def cond(carry):
        return (carry[0] >= 0) & carry[2]

    def body(carry):
        t, state, _ = carry
        state = sweep(pl.multiple_of(t * tile, tile), tile, state, None)
        return t - 1, state, live(state)

    _, state, _ = lax.while_loop(cond, body, (first - 1, state, live(state)))
    for g in range(heads):
        o_ref[:, cols[g]] = state[g][1].astype(BF16)


def _sb_prompt(q, k, v, batch, seq):
    nq = seq // Q_TILE
    width = ATTN_HEADS_PER_STEP * SB_HEAD_DIM
    qmap = lambda b, h, i: (b * nq + i, h)
    kvmap = lambda b, h, i: (b, h)
    return pl.pallas_call(
        functools.partial(_sb_prompt_kernel, tile=Q_TILE, heads=ATTN_HEADS_PER_STEP),
        grid=(batch, SB_HEADS // ATTN_HEADS_PER_STEP, nq),
        in_specs=[
            pl.BlockSpec((Q_TILE, width), qmap),
            pl.BlockSpec((seq, width), kvmap),
            pl.BlockSpec((seq, width), kvmap),
        ],
        out_specs=pl.BlockSpec((Q_TILE, width), qmap),
        out_shape=jax.ShapeDtypeStruct((batch * seq, SB_WIDTH), BF16),
        compiler_params=_params(("parallel", "parallel", "arbitrary")),
        name="sb_prompt",
    )(q, k, v)


def _sb_sample_kernel(q_ref, kn_ref, vn_ref, kc_ref, vc_ref, o_ref, *, t_new, past):
    n_keys = past + LANE
    s_idx = lax.broadcasted_iota(jnp.int32, (LANE, LANE), 0)
    j_idx = lax.broadcasted_iota(jnp.int32, (LANE, LANE), 1)
    sfx = jnp.where(j_idx > s_idx, 1.0, 0.0).astype(BF16)

    def all_heads(cache_ref, new_ref):
        cached = jnp.concatenate(
            [cache_ref[0, pl.ds(hd, past, stride=SB_HEADS), :] for hd in range(SB_HEADS)], axis=1)
        pad = jnp.zeros((LANE - t_new, SB_WIDTH), BF16)
        return jnp.concatenate([cached.astype(BF16), new_ref[...], pad], axis=0)

    q_rep = jnp.concatenate([q_ref[...]] * SB_HEADS, axis=0)
    row_head = lax.broadcasted_iota(jnp.int32, q_rep.shape, 0) // t_new
    col_head = lax.broadcasted_iota(jnp.int32, q_rep.shape, 1) // SB_HEAD_DIM
    q_bd = jnp.where(row_head == col_head, q_rep, jnp.zeros_like(q_rep))

    z = _dot_nt(all_heads(kc_ref, kn_ref), q_bd)
    key = lax.broadcasted_iota(jnp.int32, z.shape, 0)
    query = lax.broadcasted_iota(jnp.int32, z.shape, 1) % t_new
    vis = key < past + query
    l1m = jnp.where(vis, _log2_one_minus_beta(z), 0.0)
    hi = l1m.astype(BF16)
    lo = (l1m - hi.astype(F32)).astype(BF16)
    tails = []
    right = jnp.zeros((1, z.shape[1]), F32)
    for blk in reversed(range(n_keys // LANE)):
        rows = slice(blk * LANE, (blk + 1) * LANE)
        tails.insert(0, _dot(sfx, hi[rows]) + _dot(sfx, lo[rows]) + right)
        right = right + jnp.sum(l1m[rows], axis=0, keepdims=True)
    a = jnp.where(vis, jnp.exp2(z + l1m + jnp.concatenate(tails, axis=0)), 0.0)
    out = _dot(a.T.astype(BF16), all_heads(vc_ref, vn_ref))
    for hd in range(SB_HEADS):
        cols = slice(hd * SB_HEAD_DIM, (hd + 1) * SB_HEAD_DIM)
        o_ref[:, cols] = out[hd * t_new:(hd + 1) * t_new, cols].astype(BF16)


def _sb_sample(q, k_new, v_new, cache_k, cache_v, streams, t_new, past):
    assert SB_HEADS * t_new == LANE and t_new % 16 == 0 and past % LANE == 0
    row = lambda b: (b, 0)
    cmap = lambda b: (b, 0, 0)
    return pl.pallas_call(
        functools.partial(_sb_sample_kernel, t_new=t_new, past=past),
        grid=(streams,),
        in_specs=[
            pl.BlockSpec((t_new, SB_WIDTH), row),
            pl.BlockSpec((t_new, SB_WIDTH), row),
            pl.BlockSpec((t_new, SB_WIDTH), row),
            pl.BlockSpec((1, past * SB_HEADS, SB_HEAD_DIM), cmap),
            pl.BlockSpec((1, past * SB_HEADS, SB_HEAD_DIM), cmap),
        ],
        out_specs=pl.BlockSpec((t_new, SB_WIDTH), row),
        out_shape=jax.ShapeDtypeStruct((streams * t_new, SB_WIDTH), BF16),
        compiler_params=_params(("parallel",)),
        name="sb_sample",
    )(q, k_new, v_new, cache_k, cache_v)


def _softmax_steps(qs, kcs, v_ones, state, vis):
    scores = [_dot_nt(q, kc) for q, kc in zip(qs, kcs)]
    if vis is not None:
        scores = [jnp.where(vis, s, -jnp.inf) for s in scores]
    m_new = [jnp.maximum(m, jnp.max(s, axis=-1, keepdims=True)) for (m, _), s in zip(state, scores)]
    probs = [jnp.exp2(s - mn).astype(BF16) for s, mn in zip(scores, m_new)]
    return tuple((mn, acc * jnp.exp2(m - mn) + _dot(p, vo))
                 for (m, acc), mn, p, vo in zip(state, m_new, probs, v_ones))


def _mla_prompt_kernel(q_ref, kn_ref, kr_ref, v_ref, o_ref, *, tile, heads):
    i = pl.program_id(2)
    qs = [q_ref[:, 2 * LANE * g:2 * LANE * (g + 1)] for g in range(heads)]
    cols = [slice(g * NOPE_DIM, (g + 1) * NOPE_DIM) for g in range(heads)]
    ones = jnp.ones((tile, V_DIM), BF16)

    def sweep(s0, state, vis):
        k_rope = kr_ref[pl.ds(s0, tile), :]
        kcs = [jnp.concatenate([kn_ref[pl.ds(s0, tile), cg], k_rope], axis=1) for cg in cols]
        v_ones = [jnp.concatenate([v_ref[pl.ds(s0, tile), cg], ones], axis=1) for cg in cols]
        return _softmax_steps(qs, kcs, v_ones, state, vis)

    init = tuple((jnp.full((tile, 1), -jnp.inf, F32), jnp.zeros((tile, 2 * V_DIM), F32)) for _ in range(heads))
    state = lax.fori_loop(0, i, lambda t, st: sweep(pl.multiple_of(t * tile, tile), st, None), init)
    r = lax.broadcasted_iota(jnp.int32, (tile, tile), 0)
    c = lax.broadcasted_iota(jnp.int32, (tile, tile), 1)
    state = sweep(pl.multiple_of(i * tile, tile), state, (c // CHUNK) <= (r // CHUNK))
    for g in range(heads):
        acc = state[g][1]
        o_ref[:, cols[g]] = (acc[:, 0:V_DIM] / acc[:, V_DIM:2 * V_DIM]).astype(BF16)


def _mla_prompt(q_cat, k_nope, kr_pad, v, batch, seq):
    nq = seq // MLA_TILE
    hps = MLA_HEADS_PER_STEP
    qmap = lambda b, h, i: (b * nq + i, h)
    kvmap = lambda b, h, i: (b, h)
    return pl.pallas_call(
        functools.partial(_mla_prompt_kernel, tile=MLA_TILE, heads=hps),
        grid=(batch, MLA_HEADS // hps, nq),
        in_specs=[
            pl.BlockSpec((MLA_TILE, 2 * LANE * hps), qmap),
            pl.BlockSpec((seq, NOPE_DIM * hps), kvmap),
            pl.BlockSpec((seq, LANE), lambda b, h, i: (b, 0)),
            pl.BlockSpec((seq, V_DIM * hps), kvmap),
        ],
        out_specs=pl.BlockSpec((MLA_TILE, V_DIM * hps), qmap),
        out_shape=jax.ShapeDtypeStruct((batch * seq, MLA_WIDTH), BF16),
        compiler_params=_params(("parallel", "parallel", "arbitrary")),
        name="mla_prompt",
    )(q_cat, k_nope, kr_pad, v)


def _mla_sample_kernel(q_ref, knn_ref, krn_ref, vn_ref, ckv_ref, krc_ref, wuk_ref, wuv_ref, o_ref, *, t_new):
    ckv = ckv_ref[0].astype(BF16)
    k_past = _dot(ckv, wuk_ref[...]).astype(BF16)
    v_past = _dot(ckv, wuv_ref[...]).astype(BF16)
    kr_past = krc_ref[0].astype(BF16)
    kr_new = krn_ref[...]
    heads = range(MLA_HEADS)
    cols = [slice(hd * NOPE_DIM, (hd + 1) * NOPE_DIM) for hd in heads]
    qs = [q_ref[:, 2 * LANE * hd:2 * LANE * (hd + 1)] for hd in heads]
    s_past = [_dot_nt(qs[hd][:, 0:NOPE_DIM], k_past[:, cols[hd]])
              + _dot_nt(qs[hd][:, NOPE_DIM:NOPE_DIM + ROPE_DIM], kr_past) for hd in heads]
    s_new = [_dot_nt(qs[hd], jnp.concatenate([knn_ref[:, cols[hd]], kr_new], axis=1)) for hd in heads]
    ms = [jnp.maximum(jnp.max(sp, axis=-1, keepdims=True), jnp.max(sn, axis=-1, keepdims=True))
          for sp, sn in zip(s_past, s_new)]
    p_past = [jnp.exp2(sp - m) for sp, m in zip(s_past, ms)]
    p_new = [jnp.exp2(sn - m) for sn, m in zip(s_new, ms)]
    ls = [jnp.sum(pp, axis=-1, keepdims=True) + jnp.sum(pn, axis=-1, keepdims=True)
          for pp, pn in zip(p_past, p_new)]
    accs = [_dot(p_past[hd].astype(BF16), v_past[:, cols[hd]]) + _dot(p_new[hd].astype(BF16), vn_ref[:, cols[hd]])
            for hd in heads]
    for hd in heads:
        o_ref[:, cols[hd]] = (accs[hd] / ls[hd]).astype(BF16)


def _mla_sample(q_cat, kn_new, kr_new, v_new, cache_ckv, cache_kr, w_uk, w_uv, streams, t_new, past):
    row = lambda b: (b, 0)
    cmap = lambda b: (b, 0, 0)
    const = lambda b: (0, 0)
    return pl.pallas_call(
        functools.partial(_mla_sample_kernel, t_new=t_new),
        grid=(streams,),
        in_specs=[
            pl.BlockSpec((t_new, 2 * LANE * MLA_HEADS), row),
            pl.BlockSpec((t_new, MLA_HEADS * NOPE_DIM), row),
            pl.BlockSpec((t_new, LANE), row),
            pl.BlockSpec((t_new, MLA_WIDTH), row),
            pl.BlockSpec((1, past, KV_LORA), cmap),
            pl.BlockSpec((1, past, ROPE_DIM), cmap),
            pl.BlockSpec((KV_LORA, MLA_HEADS * NOPE_DIM), const),
            pl.BlockSpec((KV_LORA, MLA_WIDTH), const),
        ],
        out_specs=pl.BlockSpec((t_new, MLA_WIDTH), row),
        out_shape=jax.ShapeDtypeStruct((streams * t_new, MLA_WIDTH), BF16),
        compiler_params=_params(("parallel",)),
        name="mla_sample",
    )(q_cat, kn_new, kr_new, v_new, cache_ckv, cache_kr, w_uk, w_uv)


def _merge_ln_kernel(h_ref, osb_ref, omla_ref, wgs_ref, wgm_ref, bgs_ref, bgm_ref, wbs_ref, wbm_ref,
                     wo_ref, g_ref, b_ref, o_ref, hb_ref):
    c = pl.program_id(1)
    last = pl.num_programs(1) - 1

    @pl.when(c == 0)
    def _():
        hb_ref[...] = h_ref[...].astype(BF16)
        o_ref[...] = jnp.zeros_like(o_ref)

    hb = hb_ref[...]
    g_sb = jax.nn.sigmoid(_dot_nt(hb, wgs_ref[...]) + bgs_ref[...])
    g_mla = jax.nn.sigmoid(_dot_nt(hb, wgm_ref[...]) + bgm_ref[...])
    merged = g_sb * _dot(osb_ref[...], wbs_ref[...]) + g_mla * _dot(omla_ref[...], wbm_ref[...])
    o_ref[...] += _dot(merged.astype(BF16), wo_ref[...])

    @pl.when(c == last)
    def _():
        o_ref[...] = _layer_norm(ALPHA * h_ref[...] + o_ref[...], g_ref[...], b_ref[...])


def _merge_ln(h, o_sb, o_mla, w_in_t, b_gs, b_gm, w_bs, w_bm, w_o, g, b, bm):
    m = h.shape[0]
    nc = D_MODEL // MERGE_CHUNK
    row = lambda i, c: (i, 0)
    col = lambda i, c: (0, c)
    const = lambda i, c: (0, 0)
    sub = 16
    gate_row0 = 3 * SB_WIDTH + Q_LORA + KV_LORA + ROPE_DIM
    assert gate_row0 % sub == 0 and MERGE_CHUNK % sub == 0
    gate_rows = lambda base: pl.BlockSpec(
        (pl.Element(MERGE_CHUNK), pl.Element(D_MODEL)),
        lambda i, c: ((base // sub + c * (MERGE_CHUNK // sub)) * sub, 0))
    return pl.pallas_call(
        _merge_ln_kernel,
        grid=(m // bm, nc),
        in_specs=[
            pl.BlockSpec((bm, D_MODEL), row),
            pl.BlockSpec((bm, SB_WIDTH), row),
            pl.BlockSpec((bm, MLA_WIDTH), row),
            gate_rows(gate_row0),
            gate_rows(gate_row0 + D_MODEL),
            pl.BlockSpec((1, MERGE_CHUNK), col),
            pl.BlockSpec((1, MERGE_CHUNK), col),
            pl.BlockSpec((SB_WIDTH, MERGE_CHUNK), col),
            pl.BlockSpec((MLA_WIDTH, MERGE_CHUNK), col),
            pl.BlockSpec((MERGE_CHUNK, D_MODEL), lambda i, c: (c, 0)),
            pl.BlockSpec((1, D_MODEL), const),
            pl.BlockSpec((1, D_MODEL), const),
        ],
        out_specs=pl.BlockSpec((bm, D_MODEL), row),
        out_shape=jax.ShapeDtypeStruct((m, D_MODEL), F32),
        scratch_shapes=[pltpu.VMEM((bm, D_MODEL), BF16)],
        compiler_params=_params(("parallel", "arbitrary")),
        name="merge_ln",
    )(h, o_sb, o_mla, w_in_t, w_in_t, b_gs, b_gm, w_bs, w_bm, w_o, g, b)


def _prep_weights(b_gate, w_uq, w_ukv):
    b_gs = b_gate[:D_MODEL].reshape(1, D_MODEL)
    b_gm = b_gate[D_MODEL:].reshape(1, D_MODEL)
    w_uq3 = w_uq.astype(BF16).reshape(Q_LORA, MLA_HEADS, NOPE_DIM + ROPE_DIM)
    w_uq_p = jnp.pad(w_uq3, ((0, 0), (0, 0), (0, 2 * LANE - NOPE_DIM - ROPE_DIM))).reshape(Q_LORA, 2 * LANE * MLA_HEADS)
    w_ukv3 = w_ukv.astype(BF16).reshape(KV_LORA, MLA_HEADS, NOPE_DIM + V_DIM)
    w_uk = w_ukv3[:, :, :NOPE_DIM].reshape(KV_LORA, MLA_HEADS * NOPE_DIM)
    w_uv = w_ukv3[:, :, NOPE_DIM:].reshape(KV_LORA, MLA_WIDTH)
    return dict(b_gs=b_gs, b_gm=b_gm, w_uq=w_uq_p, w_uk=w_uk, w_uv=w_uv)


def _rope_freq():
    inv_freq = ROPE_THETA ** (-jnp.arange(0, ROPE_DIM, 2, dtype=F32) / ROPE_DIM)
    return jnp.concatenate([inv_freq, inv_freq, jnp.zeros((LANE - ROPE_DIM,), F32)]).reshape(1, LANE)


def _row2(v):
    return v.reshape(1, -1)


def kernel(x_prompt, x_sample, cache_sb_k, cache_sb_v, cache_mla_ckv, cache_mla_krope, ffn1_w_in, ffn1_w_out, ln1_g, ln1_b, w_in, b_gate, g_cq, w_uq, g_ckv, w_ukv, w_br_sb, w_br_mla, w_o, ln2_g, ln2_b, ffn2_w_in, ffn2_w_out, ln3_g, ln3_b):
    assert ffn1_w_in.shape[0] == DEPTH == 1
    b_p, t_p, _ = x_prompt.shape
    b_s, t_s, _ = x_sample.shape
    past = cache_sb_k.shape[2]
    w = _prep_weights(b_gate[0], w_uq[0], w_ukv[0])
    freq = _rope_freq()
    ln1 = (_row2(ln1_g[0]), _row2(ln1_b[0]))
    ln2 = (_row2(ln2_g[0]), _row2(ln2_b[0]))
    ln3 = (_row2(ln3_g[0]), _row2(ln3_b[0]))
    g_cq2, g_ckv2 = _row2(g_cq[0]), _row2(g_ckv[0])
    ffn1 = (ffn1_w_in[0].astype(BF16), ffn1_w_out[0].astype(BF16))

    m_p = b_p * t_p
    later = [(ffn2_w_in[0], 16), (ffn2_w_out[0], 32), (w_in[0].T, 48), (w_o[0], 16),
             (w_br_sb[0], 16), (w_br_mla[0], 16)]
    h1p, (ffn2_wi, ffn2_wo, w["w_in_t"], w["w_o"], w["w_bs"], w["w_bm"]) = _ffn_ln(
        x_prompt.reshape(m_p, D_MODEL), *ffn1, *ln1, ROW_TILE, FF_CHUNK, cast=later)
    ffn2 = (ffn2_wi, ffn2_wo)

    def ff_chunk(m):
        return FF_CHUNK if m >= ROW_TILE else FF_CHUNK_FEW_ROWS

    def projections(h1, period, offset):
        m = h1.shape[0]
        q, k, v, kb, vb = _qkv_proj(h1, w["w_in_t"], min(m, ROW_TILE_SMALL))
        qc, ckv, kr, krb, kn, vm = _lat_proj(h1, w["w_in_t"], g_cq2, g_ckv2, w["w_uq"], w["w_uk"], w["w_uv"],
                                             freq, min(m, ROW_TILE), period, offset)
        return (q, kb, vb), (qc, kn, krb, vm), (k, v, ckv, kr)

    def rowwise_back(h1, o_sb, o_mla):
        m = h1.shape[0]
        h2 = _merge_ln(h1, o_sb, o_mla, w["w_in_t"], w["b_gs"], w["b_gm"], w["w_bs"], w["w_bm"],
                       w["w_o"], *ln2, min(m, ROW_TILE))
        return _ffn_ln(h2, *ffn2, *ln3, min(m, ROW_TILE), ff_chunk(m))[0]

    (q, kb, vb), (qc, kn, krb, vm), rows_p = projections(h1p, t_p, 0)
    o_sb = _sb_prompt(q, kb, vb, b_p, t_p)
    o_mla = _mla_prompt(qc, kn, krb, vm, b_p, t_p)
    y_p = rowwise_back(h1p, o_sb, o_mla)

    m_s = b_s * t_s
    h1s = _ffn_ln(x_sample.reshape(m_s, D_MODEL), *ffn1, *ln1, m_s, ff_chunk(m_s))[0]
    (q, kb, vb), (qc, kn, krb, vm), rows_s = projections(h1s, t_s, past)
    o_sb = _sb_sample(q, kb, vb, cache_sb_k.reshape(b_s, past * SB_HEADS, SB_HEAD_DIM),
                      cache_sb_v.reshape(b_s, past * SB_HEADS, SB_HEAD_DIM), b_s, t_s, past)
    o_mla = _mla_sample(qc, kn, krb, vm, cache_mla_ckv[0], cache_mla_krope[0], w["w_uk"], w["w_uv"], b_s, t_s, past)
    y_s = rowwise_back(h1s, o_sb, o_mla)

    def cache_rows(rows, b, t):
        k, v, ckv, kr = rows
        return (k.reshape(1, b, t, SB_HEADS, SB_HEAD_DIM), v.reshape(1, b, t, SB_HEADS, SB_HEAD_DIM),
                ckv.reshape(1, b, t, KV_LORA), kr.reshape(1, b, t, ROPE_DIM))

    return (y_p.reshape(b_p, t_p, D_MODEL), y_s.reshape(b_s, t_s, D_MODEL),
            *cache_rows(rows_p, b_p, t_p), *cache_rows(rows_s, b_s, t_s))
```
